```python
import math
import jax, jax.numpy as jnp
from jax import lax
import numpy as np

D_MODEL = 1024
BATCH = 8
SEQ = 2048
DEPTH = 2

MIX_WIDTH = D_MODEL
M_HEADS = 4
M_HEAD_DIM = (MIX_WIDTH // 2) // M_HEADS
M_WIDTH = M_HEADS * M_HEAD_DIM
A_HEADS = 8
A_HEAD_DIM = (MIX_WIDTH - M_WIDTH) // A_HEADS
A_WIDTH = A_HEADS * A_HEAD_DIM
D_FF = 4 * D_MODEL
CONV_WIDTH = 5
M_CHUNK = 64
DILATED_PATTERNS = ((128, 1), (512, 4), (2048, 16))
BAND_BLOCK = 64
ROPE_THETA = 10000.0
NORM_EPS = 1e-6
NEG_INF = -1e30
IN_SPLITS = (M_WIDTH, M_WIDTH, M_WIDTH, M_WIDTH, 2 * M_HEADS, 2 * M_HEADS, A_WIDTH, A_WIDTH, A_WIDTH)
IN_WIDTH = 4 * M_WIDTH + 4 * M_HEADS + 3 * A_WIDTH

kernel_name = "hybrid_mlstm_dilated_attn_encoder"


def rms_norm(x, g):
    xf = x.astype(jnp.float32)
    y = xf * lax.rsqrt(jnp.mean(xf * xf, axis=-1, keepdims=True) + NORM_EPS)
    return (y * g.astype(jnp.float32)).astype(x.dtype)


def rope(t):
    S, D = t.shape[1], t.shape[-1]
    half = D // 2
    inv_freq = ROPE_THETA ** (-jnp.arange(half, dtype=jnp.float32) / half)
    ang = jnp.arange(S, dtype=jnp.float32)[:, None] * inv_freq[None, :]
    cos = jnp.cos(ang)[None, :, None, :]
    sin = jnp.sin(ang)[None, :, None, :]
    tf = t.astype(jnp.float32)
    t1, t2 = tf[..., :half], tf[..., half:]
    return jnp.concatenate([t1 * cos - t2 * sin, t1 * sin + t2 * cos], axis=-1)


def centred_depthwise_conv(x, w):
    K, C = w.shape
    return lax.conv_general_dilated(
        x, w[:, None, :], window_strides=(1,), padding=[(K // 2, K // 2)],
        dimension_numbers=("NWC", "WIO", "NWC"), feature_group_count=C)


def mlstm_chunkwise(q, k, v, li, lf):
    B, H, S, D = q.shape
    L = M_CHUNK
    NC = S // L
    q = q.reshape(B, H, NC, L, D)
    k = k.reshape(B, H, NC, L, D)
    v = v.reshape(B, H, NC, L, D)
    li = li.reshape(B, H, NC, L)
    lf = lf.reshape(B, H, NC, L)
    b = jnp.cumsum(lf, axis=-1)
    b_last = b[..., -1]
    g = b_last[..., None] - b + li
    g_max = jnp.max(g, axis=-1)

    def step(carry, inp):
        C, n, m = carry
        k_c, v_c, g_c, gmax_c, bl_c = inp
        m_new = jnp.maximum(bl_c + m, gmax_c)
        decay = jnp.exp(bl_c + m - m_new)
        w = jnp.exp(g_c - m_new[..., None])
        C_new = decay[..., None, None] * C + jnp.einsum("bhs,bhsk,bhsv->bhkv", w, k_c, v_c)
        n_new = decay[..., None] * n + jnp.einsum("bhs,bhsk->bhk", w, k_c)
        return (C_new, n_new, m_new), (C, n, m)

    init = (jnp.zeros((B, H, D, D), jnp.float32), jnp.zeros((B, H, D), jnp.float32),
            jnp.full((B, H), NEG_INF, jnp.float32))
    xs = (jnp.moveaxis(k, 2, 0), jnp.moveaxis(v, 2, 0), jnp.moveaxis(g, 2, 0),
          jnp.moveaxis(g_max, 2, 0), jnp.moveaxis(b_last, 2, 0))
    _, (C_prev, n_prev, m_prev) = lax.scan(step, init, xs)
    C_prev = jnp.moveaxis(C_prev, 0, 2)
    n_prev = jnp.moveaxis(n_prev, 0, 2)
    m_prev = jnp.moveaxis(m_prev, 0, 2)

    a = b + m_prev[..., None]
    lower = jnp.tril(jnp.ones((L, L), dtype=bool))
    dmat = jnp.where(lower, b[..., :, None] - b[..., None, :] + li[..., None, :], NEG_INF)
    m_t = jnp.maximum(a, jnp.max(dmat, axis=-1))
    s = jnp.einsum("bhnqd,bhnkd->bhnqk", q, k) * jnp.exp(dmat - m_t[..., None])
    inter = jnp.exp(a - m_t)
    num = (jnp.einsum("bhnqk,bhnkd->bhnqd", s, v)
           + inter[..., None] * jnp.einsum("bhnqk,bhnkv->bhnqv", q, C_prev))
    den = jnp.sum(s, axis=-1) + inter * jnp.einsum("bhnqk,bhnk->bhnq", q, n_prev)
    h = num / jnp.maximum(jnp.abs(den), jnp.exp(-m_t))[..., None]
    return h.reshape(B, H, S, D)


def strided_band_attention(q, k, v, dilation, half):
    B, H, S, D = q.shape
    L = S // dilation
    qb_len = math.gcd(L, BAND_BLOCK)
    nb = L // qb_len
    span = qb_len + 2 * half

    def to_strided(t):
        return t.reshape(B, H, L, dilation, D).transpose(0, 1, 3, 2, 4)

    qs = to_strided(q).reshape(B, H, dilation, nb, qb_len, D)
    pad = ((0, 0), (0, 0), (0, 0), (half, half), (0, 0))
    kp = jnp.pad(to_strided(k), pad)
    vp = jnp.pad(to_strided(v), pad)
    idx = (jnp.arange(nb) * qb_len)[:, None] + jnp.arange(span)[None, :]
    kb = kp[:, :, :, idx, :]
    vb = vp[:, :, :, idx, :]
    qpos = (jnp.arange(nb) * qb_len)[:, None, None] + jnp.arange(qb_len)[None, :, None]
    kpos = idx[:, None, :] - half
    valid = (jnp.abs(kpos - qpos) <= half) & (kpos >= 0) & (kpos < L)
    s = jnp.einsum("bhrnqd,bhrnkd->bhrnqk", qs, kb) * (1.0 / math.sqrt(D))
    s = jnp.where(valid, s, NEG_INF)
    m = jnp.max(s, axis=-1)
    p = jnp.exp(s - m[..., None])
    l = jnp.sum(p, axis=-1)
    o = jnp.einsum("bhrnqk,bhrnkd->bhrnqd", p, vb) / l[..., None]

    def from_strided(t):
        tail = t.shape[5:]
        t = jnp.moveaxis(t.reshape((B, H, dilation, L) + tail), 2, 3)
        return t.reshape((B, H, S) + tail)

    return from_strided(o), from_strided(m), from_strided(l)


def dilated_attention(q, k, v):
    outs = [strided_band_attention(q, k, v, d, w // (2 * d)) for (w, d) in DILATED_PATTERNS]
    m_all = jnp.max(jnp.stack([m for (_, m, _) in outs]), axis=0)
    weights = [l * jnp.exp(m - m_all) for (_, m, l) in outs]
    num = sum(w[..., None] * o for w, (o, _, _) in zip(weights, outs))
    return num / sum(weights)[..., None]


def hybrid_layer(x, norm1_g, w_in, conv_w, gate_i_b, gate_f_b, head_norm_g, w_out,
                 norm2_g, w_up, w_down):
    B, S, _ = x.shape
    h = rms_norm(x, norm1_g)
    proj = h @ w_in
    mq, mk, mv, mo, mi, mf, aq, ak, av = jnp.split(
        proj, list(np.cumsum(IN_SPLITS)[:-1]), axis=-1)

    qk = jax.nn.silu(centred_depthwise_conv(jnp.concatenate([mq, mk], axis=-1), conv_w))
    mq, mk = qk[..., :M_WIDTH], qk[..., M_WIDTH:]

    def m_heads(t):
        return t.reshape(B, S, M_HEADS, M_HEAD_DIM).transpose(0, 2, 1, 3).astype(jnp.float32)

    q_m = m_heads(mq)
    k_m = m_heads(mk) * (1.0 / math.sqrt(M_HEAD_DIM))
    v_m = m_heads(mv)
    gi = (mi + gate_i_b).astype(jnp.float32).reshape(B, S, 2, M_HEADS).transpose(2, 0, 3, 1)
    gf = jax.nn.log_sigmoid((mf + gate_f_b).astype(jnp.float32)).reshape(
        B, S, 2, M_HEADS).transpose(2, 0, 3, 1)
    h_fwd = mlstm_chunkwise(q_m, k_m, v_m, gi[0], gf[0])
    flip = lambda t: jnp.flip(t, axis=2)
    h_bwd = flip(mlstm_chunkwise(flip(q_m), flip(k_m), flip(v_m), flip(gi[1]), flip(gf[1])))
    hm = h_fwd + h_bwd
    hm = hm * lax.rsqrt(jnp.mean(hm * hm, axis=-1, keepdims=True) + NORM_EPS)
    hm = hm.transpose(0, 2, 1, 3).reshape(B, S, M_WIDTH) * head_norm_g.astype(jnp.float32)
    y_m = (hm * jax.nn.sigmoid(mo.astype(jnp.float32))).astype(x.dtype)

    def a_heads(t, rotate):
        t = t.reshape(B, S, A_HEADS, A_HEAD_DIM)
        t = rope(t) if rotate else t.astype(jnp.float32)
        return t.transpose(0, 2, 1, 3)

    o_a = dilated_attention(a_heads(aq, True), a_heads(ak, True), a_heads(av, False))
    y_a = o_a.transpose(0, 2, 1, 3).reshape(B, S, A_WIDTH).astype(x.dtype)

    x = x + jnp.concatenate([y_m, y_a], axis=-1) @ w_out

    u = jax.nn.relu(rms_norm(x, norm2_g) @ w_up)
    return x + (u * u) @ w_down


def setup_inputs(seed: int = 0) -> dict:
    key = jax.random.key(seed)
    ks = jax.random.split(key, 13)
    f32 = jnp.float32
    nrm = lambda k, shape: jax.random.normal(k, shape, f32)
    x = nrm(ks[0], (BATCH, SEQ, D_MODEL))
    norm1_g = 1.0 + 0.02 * nrm(ks[1], (DEPTH, D_MODEL))
    w_in = nrm(ks[2], (DEPTH, D_MODEL, IN_WIDTH)) * D_MODEL ** -0.5
    conv_w = nrm(ks[3], (DEPTH, CONV_WIDTH, 2 * M_WIDTH)) * CONV_WIDTH ** -0.5
    gate_i_b = 0.1 * nrm(ks[4], (DEPTH, 2 * M_HEADS))
    gate_f_b = (jnp.tile(jnp.linspace(3.0, 6.0, M_HEADS, dtype=f32), 2)[None, :]
                + 0.1 * nrm(ks[5], (DEPTH, 2 * M_HEADS)))
    head_norm_g = 1.0 + 0.02 * nrm(ks[6], (DEPTH, M_WIDTH))
    w_out = nrm(ks[7], (DEPTH, MIX_WIDTH, D_MODEL)) * MIX_WIDTH ** -0.5
    norm2_g = 1.0 + 0.02 * nrm(ks[8], (DEPTH, D_MODEL))
    w_up = nrm(ks[9], (DEPTH, D_MODEL, D_FF)) * D_MODEL ** -0.5
    w_down = nrm(ks[10], (DEPTH, D_FF, D_MODEL)) * D_FF ** -0.5
    final_g = 1.0 + 0.02 * nrm(ks[11], (D_MODEL,))
    return {"x": x, "norm1_g": norm1_g, "w_in": w_in, "conv_w": conv_w,
            "gate_i_b": gate_i_b, "gate_f_b": gate_f_b, "head_norm_g": head_norm_g,
            "w_out": w_out, "norm2_g": norm2_g, "w_up": w_up, "w_down": w_down,
            "final_g": final_g}


def reference(x, norm1_g, w_in, conv_w, gate_i_b, gate_f_b, head_norm_g, w_out,
              norm2_g, w_up, w_down, final_g):
    for layer in range(DEPTH):
        x = hybrid_layer(x, norm1_g[layer], w_in[layer], conv_w[layer], gate_i_b[layer],
                         gate_f_b[layer], head_norm_g[layer], w_out[layer], norm2_g[layer],
                         w_up[layer], w_down[layer])
    return rms_norm(x, final_g)
```

```python
import functools
import math

import jax
import jax.numpy as jnp
from jax import lax
from jax.experimental import pallas as pl
from jax.experimental.pallas import tpu as pltpu

F32 = jnp.float32
BF16 = jnp.bfloat16

D_MODEL = 1024
M_HEADS = 4
M_HEAD_DIM = 128
M_WIDTH = M_HEADS * M_HEAD_DIM
A_HEADS = 8
A_HEAD_DIM = 64
A_WIDTH = A_HEADS * A_HEAD_DIM
D_FF = 4 * D_MODEL
CONV_WIDTH = 5
ROPE_THETA = 10000.0
NORM_EPS = 1e-6
NEG_INF = -1e30
DILATIONS = (1, 4, 16)
HALF_WINDOW = 64

LANES = 128
N_GATES = 4 * M_HEADS
MAIN_COLS = 4 * M_WIDTH + 3 * A_WIDTH
PROJ_COLS = MAIN_COLS + LANES
CHUNK = 128
QB = 128
SPAN = QB + 2 * HALF_WINDOW
VMEM_LIMIT = 56 * 1024 * 1024


def _sigmoid(x):
    return 1.0 / (1.0 + jnp.exp(-x))


def _in_proj_kernel(x_ref, g_ref, w_ref, o_ref, *, col_chunk):
    x = x_ref[...]
    ms = jnp.mean(x * x, axis=-1, keepdims=True)
    h = (x * lax.rsqrt(ms + NORM_EPS) * g_ref[...]).astype(BF16)
    n = w_ref.shape[1]
    for c in range(0, n, col_chunk):
        w = min(col_chunk, n - c)
        o_ref[:, c:c + w] = jnp.dot(h, w_ref[:, c:c + w], preferred_element_type=F32)


def _in_proj(x2d, g, w, tm=512):
    m = x2d.shape[0]
    n = w.shape[1]
    return pl.pallas_call(
        functools.partial(_in_proj_kernel, col_chunk=512),
        grid=(m // tm,),
        in_specs=[
            pl.BlockSpec((tm, D_MODEL), lambda i: (i, 0)),
            pl.BlockSpec((1, D_MODEL), lambda i: (0, 0)),
            pl.BlockSpec((D_MODEL, n), lambda i: (0, 0), pipeline_mode=pl.Buffered(1)),
        ],
        out_specs=pl.BlockSpec((tm, n), lambda i: (i, 0)),
        out_shape=jax.ShapeDtypeStruct((m, n), F32),
        compiler_params=pltpu.CompilerParams(
            dimension_semantics=("arbitrary",), vmem_limit_bytes=VMEM_LIMIT),
        name="in_proj",
    )(x2d, g, w)


def _lane_cumsum(x, reverse):
    lane = lax.broadcasted_iota(jnp.int32, x.shape, 1)
    k = 1
    while k < LANES:
        if reverse:
            x = x + jnp.where(lane < LANES - k, pltpu.roll(x, LANES - k, axis=1), 0.0)
        else:
            x = x + jnp.where(lane >= k, pltpu.roll(x, k, axis=1), 0.0)
        k *= 2
    return x


def _mlstm_kernel(bias_ref, q_ref, k_ref, v_ref, o_ref, li_f_ref, li_b_ref, lf_f_ref, lf_b_ref,
                  cwq_ref, cwk_ref, hg_ref, y_ref,
                  pad_ref, qc_ref, kc_ref, va_ref, hf_ref, hb_ref, cst_ref, gl_ref):
    seq = q_ref.shape[0]
    n_chunks = seq // CHUNK
    head = pl.program_id(1)

    pad_ref[0:8, :] = jnp.zeros((8, LANES), F32)
    pad_ref[8 + seq:16 + seq, :] = jnp.zeros((8, LANES), F32)

    def conv_into(src_ref, cw_ref, dst_ref, scale):
        pad_ref[8:8 + seq, :] = src_ref[...]

        def body(c, carry):
            r0 = pl.multiple_of(c * CHUNK, CHUNK)
            acc = jnp.zeros((CHUNK, LANES), F32)
            for t in range(CONV_WIDTH):
                acc = acc + cw_ref[t:t + 1, :] * pad_ref[pl.ds(r0 + 8 + t - CONV_WIDTH // 2, CHUNK), :]
            act = acc * _sigmoid(acc)
            if scale != 1.0:
                act = act * scale
            dst_ref[pl.ds(r0, CHUNK), :] = act.astype(dst_ref.dtype)
            return carry

        lax.fori_loop(0, n_chunks, body, 0)

    conv_into(q_ref, cwq_ref, qc_ref, 1.0)
    conv_into(k_ref, cwk_ref, kc_ref, 1.0 / math.sqrt(M_HEAD_DIM))

    va_ref[:, 0:LANES] = v_ref[...].astype(BF16)
    va_ref[:, LANES:2 * LANES] = jnp.ones((seq, LANES), BF16)

    for d, (li_ref, lf_ref) in enumerate(((li_f_ref, lf_f_ref), (li_b_ref, lf_b_ref))):
        li = li_ref[...] + bias_ref[d * M_HEADS + head]
        fpre = lf_ref[...] + bias_ref[2 * M_HEADS + d * M_HEADS + head]
        lf = jnp.minimum(fpre, 0.0) - jnp.log(1.0 + jnp.exp(-jnp.abs(fpre)))
        b = _lane_cumsum(lf, reverse=(d == 1))
        u = li - b
        tot = jnp.broadcast_to(jnp.sum(lf, axis=-1, keepdims=True), lf.shape)
        umax = jnp.broadcast_to(jnp.max(u, axis=-1, keepdims=True), lf.shape)
        gl_ref[d, 0] = u
        gl_ref[d, 1] = lf
        gl_ref[d, 2] = tot
        gl_ref[d, 3] = umax

    cst_ref[...] = jnp.zeros(cst_ref.shape, F32)

    row = lax.broadcasted_iota(jnp.int32, (CHUNK, CHUNK), 0)
    col = lax.broadcasted_iota(jnp.int32, (CHUNK, CHUNK), 1)
    masks = (col <= row, col >= row)

    def chunk_step(c, m_prev, d, h_ref):
        mask = masks[d]
        r0 = pl.multiple_of(c * CHUNK, CHUNK)
        q = qc_ref[pl.ds(r0, CHUNK), :]
        k = kc_ref[pl.ds(r0, CHUNK), :]
        va = va_ref[pl.ds(r0, CHUNK), :]
        u = gl_ref[d, 0, pl.ds(c, 1), :]
        lf = gl_ref[d, 1, pl.ds(c, 1), :]
        tot = gl_ref[d, 2, pl.ds(c, 1), :]
        umax = gl_ref[d, 3, pl.ds(c, 1), :]

        s = lax.dot_general(q, k.astype(BF16), (((1,), (1,)), ((), ())), preferred_element_type=F32)
        um = jnp.where(mask, u, NEG_INF)
        mloc = jnp.maximum(jnp.max(um, axis=-1, keepdims=True), m_prev)
        dt = jnp.exp(um - mloc)
        inter = jnp.exp(m_prev - mloc)
        lhs = jnp.concatenate([(s * dt).astype(BF16), (q.astype(F32) * inter).astype(BF16)], axis=1)
        state = cst_ref[d]
        rhs = jnp.concatenate([va, state.astype(BF16)], axis=0)
        out = jnp.dot(lhs, rhs, preferred_element_type=F32)
        b_col = jnp.sum(jnp.where(mask, lf, 0.0), axis=-1, keepdims=True)
        stab = jnp.exp(-(b_col + mloc))
        h_ref[pl.ds(r0, CHUNK), :] = out[:, 0:LANES] / jnp.maximum(jnp.abs(out[:, LANES:]), stab)

        m_new = jnp.maximum(tot + m_prev, umax + tot)
        decay = jnp.exp(tot + m_prev - m_new)
        w = jnp.exp(u + tot - m_new)
        ktw = (k.T * w).astype(BF16)
        upd = jnp.dot(ktw, va, preferred_element_type=F32)
        cst_ref[d] = jnp.concatenate([decay, decay], axis=1) * state + upd
        return m_new

    def scan_body(i, carry):
        m_f, m_b = carry
        m_f = chunk_step(i, m_f, 0, hf_ref)
        m_b = chunk_step(n_chunks - 1 - i, m_b, 1, hb_ref)
        return m_f, m_b

    m0 = jnp.full((1, LANES), NEG_INF, F32)
    lax.fori_loop(0, n_chunks, scan_body, (m0, m0))

    def fin_body(c, carry):
        r0 = pl.multiple_of(c * CHUNK, CHUNK)
        hm = hf_ref[pl.ds(r0, CHUNK), :] + hb_ref[pl.ds(r0, CHUNK), :]
        ms = jnp.mean(hm * hm, axis=-1, keepdims=True)
        y = hm * lax.rsqrt(ms + NORM_EPS) * hg_ref[...] * _sigmoid(o_ref[pl.ds(r0, CHUNK), :])
        y_ref[pl.ds(r0, CHUNK), :] = y.astype(y_ref.dtype)
        return carry

    lax.fori_loop(0, n_chunks, fin_body, 0)


def _mlstm(proj3, gates_row, gate_bias, conv_w, head_norm_g):
    batch, seq, _ = proj3.shape
    n_chunks = seq // CHUNK
    col = lambda off: pl.BlockSpec((None, seq, LANES), lambda b, h: (b, 0, off + h))
    gate = lambda off: pl.BlockSpec((None, None, n_chunks, LANES), lambda b, h: (b, off + h, 0, 0))
    return pl.pallas_call(
        _mlstm_kernel,
        grid=(batch, M_HEADS),
        in_specs=[
            pl.BlockSpec(memory_space=pltpu.SMEM),
            col(0), col(M_HEADS), col(2 * M_HEADS), col(3 * M_HEADS),
            gate(0), gate(M_HEADS), gate(2 * M_HEADS), gate(3 * M_HEADS),
            pl.BlockSpec((CONV_WIDTH, LANES), lambda b, h: (0, h)),
            pl.BlockSpec((CONV_WIDTH, LANES), lambda b, h: (0, M_HEADS + h)),
            pl.BlockSpec((1, LANES), lambda b, h: (0, h)),
        ],
        out_specs=pl.BlockSpec((None, seq, LANES), lambda b, h: (b, 0, h)),
        out_shape=jax.ShapeDtypeStruct((batch, seq, M_WIDTH), BF16),
        scratch_shapes=[
            pltpu.VMEM((seq + 16, LANES), F32),
            pltpu.VMEM((seq, LANES), BF16),
            pltpu.VMEM((seq, LANES), F32),
            pltpu.VMEM((seq, 2 * LANES), BF16),
            pltpu.VMEM((seq, LANES), F32),
            pltpu.VMEM((seq, LANES), F32),
            pltpu.VMEM((2, M_HEAD_DIM, 2 * LANES), F32),
            pltpu.VMEM((2, 4, n_chunks, LANES), F32),
        ],
        compiler_params=pltpu.CompilerParams(
            dimension_semantics=("arbitrary", "arbitrary"), vmem_limit_bytes=VMEM_LIMIT),
        name="mlstm",
    )(gate_bias, proj3, proj3, proj3, proj3, gates_row, gates_row, gates_row, gates_row,
      conv_w, conv_w, head_norm_g)


def _attn_kernel(q_ref, k_ref, v_ref, cos_ref, sin_ref, bias_ref, bias16_ref, y_ref,
                 qr_ref, kr_ref, num_ref, den_ref, mx_ref):
    seq = q_ref.shape[0]
    rope_rows = 256

    lane_r = lax.broadcasted_iota(jnp.int32, (rope_rows, LANES), 1)
    first_half = (lane_r % A_HEAD_DIM) < (A_HEAD_DIM // 2)

    def rope_body(c, carry):
        r0 = pl.multiple_of(c * rope_rows, rope_rows)
        rows = pl.ds(r0, rope_rows)
        cos = cos_ref[rows, :]
        sin = sin_ref[rows, :]
        for src, dst, scale in ((q_ref, qr_ref, 1.0 / math.sqrt(A_HEAD_DIM)), (k_ref, kr_ref, 1.0)):
            x = src[rows, :]
            partner = jnp.where(first_half, pltpu.roll(x, LANES - A_HEAD_DIM // 2, axis=1),
                                pltpu.roll(x, A_HEAD_DIM // 2, axis=1))
            out = x * cos + partner * sin
            if scale != 1.0:
                out = out * scale
            dst[rows, :] = out
        return carry

    lax.fori_loop(0, seq // rope_rows, rope_body, 0)

    lane_q = lax.broadcasted_iota(jnp.int32, (QB, LANES), 1)
    head0_q = lane_q < A_HEAD_DIM

    def block(pat, qrows, krows, bias, span):
        lane_k = lax.broadcasted_iota(jnp.int32, (span, LANES), 1)
        head0_k = lane_k < A_HEAD_DIM
        q = qr_ref[qrows, :]
        k = kr_ref[krows, :].astype(BF16)
        v = v_ref[krows, :]
        q2 = jnp.concatenate([jnp.where(head0_q, q, 0.0), jnp.where(head0_q, 0.0, q)], axis=0).astype(BF16)
        s = lax.dot_general(q2, k, (((1,), (1,)), ((), ())), preferred_element_type=F32)
        s = s + jnp.concatenate([bias, bias], axis=0)
        m = jnp.max(s, axis=-1, keepdims=True)
        p = jnp.exp(s - m).astype(BF16)
        ones = jnp.ones((span, LANES), F32)
        r0 = jnp.concatenate([jnp.where(head0_k, v, 0.0), jnp.where(head0_k, ones, 0.0)], axis=1).astype(BF16)
        r1 = jnp.concatenate([jnp.where(head0_k, 0.0, v), jnp.where(head0_k, 0.0, ones)], axis=1).astype(BF16)
        acc = (jnp.dot(p[0:QB], r0, preferred_element_type=F32)
               + jnp.dot(p[QB:2 * QB], r1, preferred_element_type=F32))
        num_ref[pat, qrows, :] = acc[:, 0:LANES]
        den_ref[pat, qrows, :] = acc[:, LANES:2 * LANES]
        mx_ref[pat, qrows, :] = jnp.where(head0_q, m[0:QB], m[QB:2 * QB])

    for pat, dil in enumerate(DILATIONS):
        length = seq // dil
        n_blocks = length // QB

        if n_blocks == 1:
            def body16(r, carry, pat=pat, dil=dil):
                rows = pl.ds(r, QB, stride=dil)
                block(pat, rows, rows, bias16_ref[...], QB)
                return carry
            lax.fori_loop(0, dil, body16, 0)
            continue

        def body(i, carry, pat=pat, dil=dil, length=length, n_blocks=n_blocks):
            r = i // n_blocks
            j = i % n_blocks
            q0 = j * QB
            k0 = jnp.clip(q0 - HALF_WINDOW, 0, length - SPAN)
            variant = (q0 - k0) // HALF_WINDOW
            if dil == 1:
                qrows = pl.ds(pl.multiple_of(q0, QB), QB)
                krows = pl.ds(pl.multiple_of(k0, HALF_WINDOW), SPAN)
            else:
                qrows = pl.ds(r + dil * q0, QB, stride=dil)
                krows = pl.ds(r + dil * k0, SPAN, stride=dil)
            block(pat, qrows, krows, bias_ref[variant], SPAN)
            return carry

        lax.fori_loop(0, dil * n_blocks, body, 0)

    def merge_body(c, carry):
        rows = pl.ds(pl.multiple_of(c * rope_rows, rope_rows), rope_rows)
        m_all = jnp.maximum(jnp.maximum(mx_ref[0, rows, :], mx_ref[1, rows, :]), mx_ref[2, rows, :])
        num = jnp.zeros((rope_rows, LANES), F32)
        den = jnp.zeros((rope_rows, LANES), F32)
        for pat in range(len(DILATIONS)):
            w = jnp.exp(mx_ref[pat, rows, :] - m_all)
            num = num + w * num_ref[pat, rows, :]
            den = den + w * den_ref[pat, rows, :]
        y_ref[rows, :] = (num / den).astype(y_ref.dtype)
        return carry

    lax.fori_loop(0, seq // rope_rows, merge_body, 0)


def _attention(proj3, cos_tab, sin_tab, bias, bias16):
    batch, seq, _ = proj3.shape
    n_pairs = A_WIDTH // LANES
    base = 4 * M_WIDTH // LANES
    col = lambda off: pl.BlockSpec((None, seq, LANES), lambda b, p: (b, 0, base + off + p))
    const2 = lambda shape: pl.BlockSpec(shape, lambda b, p: (0,) * len(shape))
    n_pat = len(DILATIONS)
    return pl.pallas_call(
        _attn_kernel,
        grid=(batch, n_pairs),
        in_specs=[
            col(0), col(n_pairs), col(2 * n_pairs),
            const2((seq, LANES)), const2((seq, LANES)),
            const2((3, QB, SPAN)), const2((QB, QB)),
        ],
        out_specs=pl.BlockSpec((None, seq, LANES), lambda b, p: (b, 0, p)),
        out_shape=jax.ShapeDtypeStruct((batch, seq, A_WIDTH), BF16),
        scratch_shapes=[
            pltpu.VMEM((seq, LANES), F32),
            pltpu.VMEM((seq, LANES), F32),
            pltpu.VMEM((n_pat, seq, LANES), F32),
            pltpu.VMEM((n_pat, seq, LANES), F32),
            pltpu.VMEM((n_pat, seq, LANES), F32),
        ],
        compiler_params=pltpu.CompilerParams(
            dimension_semantics=("arbitrary", "arbitrary"), vmem_limit_bytes=VMEM_LIMIT),
        name="dilated_attn",
    )(proj3, proj3, proj3, cos_tab, sin_tab, bias, bias16)


def _out_mlp_kernel(x_ref, ym_ref, ya_ref, wo_ref, g2_ref, wu_ref, wd_ref, fg_ref, o_ref, usq_ref,
                    *, ff_chunk, final):
    x1 = (x_ref[...]
          + jnp.dot(ym_ref[...], wo_ref[0:M_WIDTH, :], preferred_element_type=F32)
          + jnp.dot(ya_ref[...], wo_ref[M_WIDTH:, :], preferred_element_type=F32))
    ms = jnp.mean(x1 * x1, axis=-1, keepdims=True)
    h = (x1 * lax.rsqrt(ms + NORM_EPS) * g2_ref[...]).astype(BF16)
    for c in range(0, D_FF, ff_chunk):
        u = jnp.maximum(jnp.dot(h, wu_ref[:, c:c + ff_chunk], preferred_element_type=F32), 0.0)
        usq_ref[:, c:c + ff_chunk] = (u * u).astype(BF16)
    out = x1 + jnp.dot(usq_ref[...], wd_ref[...], preferred_element_type=F32)
    if final:
        ms = jnp.mean(out * out, axis=-1, keepdims=True)
        out = out * lax.rsqrt(ms + NORM_EPS) * fg_ref[...]
    o_ref[...] = out


def _out_mlp(x2d, ym, ya, w_out, g2, w_up, w_down, final_g, final, tm=512):
    m = x2d.shape[0]
    resident = lambda shape: pl.BlockSpec(shape, lambda i: (0, 0), pipeline_mode=pl.Buffered(1))
    return pl.pallas_call(
        functools.partial(_out_mlp_kernel, ff_chunk=1024, final=final),
        grid=(m // tm,),
        in_specs=[
            pl.BlockSpec((tm, D_MODEL), lambda i: (i, 0)),
            pl.BlockSpec((tm, M_WIDTH), lambda i: (i, 0)),
            pl.BlockSpec((tm, A_WIDTH), lambda i: (i, 0)),
            resident((D_MODEL, D_MODEL)),
            resident((1, D_MODEL)),
            resident((D_MODEL, D_FF)),
            resident((D_FF, D_MODEL)),
            resident((1, D_MODEL)),
        ],
        out_specs=pl.BlockSpec((tm, D_MODEL), lambda i: (i, 0)),
        out_shape=jax.ShapeDtypeStruct((m, D_MODEL), F32),
        scratch_shapes=[pltpu.VMEM((tm, D_FF), BF16)],
        compiler_params=pltpu.CompilerParams(
            dimension_semantics=("arbitrary",), vmem_limit_bytes=VMEM_LIMIT),
        name="out_mlp",
    )(x2d, ym, ya, w_out, g2, w_up, w_down, final_g)


def _rope_tables(seq):
    half = A_HEAD_DIM // 2
    inv_freq = ROPE_THETA ** (-jnp.arange(half, dtype=F32) / half)
    ang = jnp.arange(seq, dtype=F32)[:, None] * inv_freq[None, :]
    cos = jnp.tile(jnp.cos(ang), (1, LANES // half))
    sin = jnp.sin(ang)
    sin = jnp.tile(jnp.concatenate([-sin, sin], axis=1), (1, LANES // A_HEAD_DIM))
    return cos, sin


def _band_bias():
    row = jnp.arange(QB)[:, None]
    col = jnp.arange(SPAN)[None, :]
    variants = []
    for off in (0, HALF_WINDOW, 2 * HALF_WINDOW):
        rel = col - off - row
        variants.append(jnp.where(jnp.abs(rel) <= HALF_WINDOW, 0.0, NEG_INF))
    rel16 = jnp.arange(QB)[None, :] - row
    bias16 = jnp.where(jnp.abs(rel16) <= HALF_WINDOW, 0.0, NEG_INF)
    return jnp.stack(variants).astype(F32), bias16.astype(F32)


def kernel(x, norm1_g, w_in, conv_w, gate_i_b, gate_f_b, head_norm_g, w_out, norm2_g, w_up, w_down, final_g):
    batch, seq, _ = x.shape
    depth = w_in.shape[0]
    n_chunks = seq // CHUNK
    cos_tab, sin_tab = _rope_tables(seq)
    bias, bias16 = _band_bias()
    gate_lo = 4 * M_WIDTH

    x2d = x.reshape(batch * seq, D_MODEL)
    for layer in range(depth):
        wl = w_in[layer]
        w_cat = jnp.concatenate(
            [wl[:, :gate_lo], wl[:, gate_lo + N_GATES:], wl[:, gate_lo:gate_lo + N_GATES],
             jnp.zeros((D_MODEL, LANES - N_GATES), F32)], axis=1).astype(BF16)
        proj = _in_proj(x2d, norm1_g[layer][None, :], w_cat)
        proj3 = proj.reshape(batch, seq, PROJ_COLS)
        gates_row = jnp.transpose(proj3[:, :, MAIN_COLS:MAIN_COLS + N_GATES], (0, 2, 1)).reshape(
            batch, N_GATES, n_chunks, CHUNK)
        gate_bias = jnp.concatenate([gate_i_b[layer], gate_f_b[layer]]).astype(F32)
        ym = _mlstm(proj3, gates_row, gate_bias, conv_w[layer], head_norm_g[layer][None, :])
        ya = _attention(proj3, cos_tab, sin_tab, bias, bias16)
        x2d = _out_mlp(x2d, ym.reshape(batch * seq, M_WIDTH), ya.reshape(batch * seq, A_WIDTH),
                       w_out[layer].astype(BF16), norm2_g[layer][None, :],
                       w_up[layer].astype(BF16), w_down[layer].astype(BF16),
                       final_g[None, :], final=(layer == depth - 1))
    return x2d.reshape(batch, seq, D_MODEL)
```

```python
import functools
import math

import jax
import jax.numpy as jnp
from jax import lax
from jax.experimental import pallas as pl
from jax.experimental.pallas import tpu as pltpu

F32 = jnp.float32
BF16 = jnp.bfloat16

D_MODEL = 1024
M_HEADS = 4
M_HEAD_DIM = 128
M_WIDTH = M_HEADS * M_HEAD_DIM
A_HEADS = 8
A_HEAD_DIM = 64
A_WIDTH = A_HEADS * A_HEAD_DIM
D_FF = 4 * D_MODEL
CONV_WIDTH = 5
ROPE_THETA = 10000.0
NORM_EPS = 1e-6
NEG_INF = -1e30
DILATIONS = (1, 4, 16)
HALF_WINDOW = 64

LANES = 128
N_GATES = 4 * M_HEADS
MAIN_COLS = 4 * M_WIDTH + 3 * A_WIDTH
PROJ_COLS = MAIN_COLS + LANES
CHUNK = 128
QB = 128
SPAN = QB + 2 * HALF_WINDOW
VMEM_LIMIT = 56 * 1024 * 1024


def _sigmoid(x):
    return 1.0 / (1.0 + jnp.exp(-x))


def _in_proj_kernel(x_ref, g_ref, w_ref, o_ref, *, col_chunk):
    x = x_ref[...]
    ms = jnp.mean(x * x, axis=-1, keepdims=True)
    h = (x * lax.rsqrt(ms + NORM_EPS) * g_ref[...]).astype(BF16)
    n = w_ref.shape[1]
    for c in range(0, n, col_chunk):
        w = min(col_chunk, n - c)
        o_ref[:, c:c + w] = jnp.dot(h, w_ref[:, c:c + w], preferred_element_type=F32)


def _in_proj(x2d, g, w, tm=512):
    m = x2d.shape[0]
    n = w.shape[1]
    return pl.pallas_call(
        functools.partial(_in_proj_kernel, col_chunk=512),
        grid=(m // tm,),
        in_specs=[
            pl.BlockSpec((tm, D_MODEL), lambda i: (i, 0)),
            pl.BlockSpec((1, D_MODEL), lambda i: (0, 0)),
            pl.BlockSpec((D_MODEL, n), lambda i: (0, 0), pipeline_mode=pl.Buffered(1)),
        ],
        out_specs=pl.BlockSpec((tm, n), lambda i: (i, 0)),
        out_shape=jax.ShapeDtypeStruct((m, n), F32),
        compiler_params=pltpu.CompilerParams(
            dimension_semantics=("arbitrary",), vmem_limit_bytes=VMEM_LIMIT),
        name="in_proj",
    )(x2d, g, w)


def _lane_cumsum(x, reverse):
    lane = lax.broadcasted_iota(jnp.int32, x.shape, 1)
    k = 1
    while k < LANES:
        if reverse:
            x = x + jnp.where(lane < LANES - k, pltpu.roll(x, LANES - k, axis=1), 0.0)
        else:
            x = x + jnp.where(lane >= k, pltpu.roll(x, k, axis=1), 0.0)
        k *= 2
    return x


def _mlstm_kernel(bias_ref, q_ref, k_ref, v_ref, o_ref, li_f_ref, li_b_ref, lf_f_ref, lf_b_ref,
                  cwq_ref, cwk_ref, hg_ref, y_ref,
                  pad_ref, qc_ref, kc_ref, va_ref, hf_ref, hb_ref, cst_ref, gl_ref):
    seq = q_ref.shape[0]
    n_chunks = seq // CHUNK
    head = pl.program_id(1)

    pad_ref[0:8, :] = jnp.zeros((8, LANES), F32)
    pad_ref[8 + seq:16 + seq, :] = jnp.zeros((8, LANES), F32)

    def conv_into(src_ref, cw_ref, dst_ref, scale):
        pad_ref[8:8 + seq, :] = src_ref[...]

        def body(c, carry):
            r0 = pl.multiple_of(c * CHUNK, CHUNK)
            acc = jnp.zeros((CHUNK, LANES), F32)
            for t in range(CONV_WIDTH):
                acc = acc + cw_ref[t:t + 1, :] * pad_ref[pl.ds(r0 + 8 + t - CONV_WIDTH // 2, CHUNK), :]
            act = acc * _sigmoid(acc)
            if scale != 1.0:
                act = act * scale
            dst_ref[pl.ds(r0, CHUNK), :] = act.astype(dst_ref.dtype)
            return carry

        lax.fori_loop(0, n_chunks, body, 0)

    conv_into(q_ref, cwq_ref, qc_ref, 1.0)
    conv_into(k_ref, cwk_ref, kc_ref, 1.0 / math.sqrt(M_HEAD_DIM))

    va_ref[:, 0:LANES] = v_ref[...].astype(BF16)
    va_ref[:, LANES:2 * LANES] = jnp.ones((seq, LANES), BF16)

    for d, (li_ref, lf_ref) in enumerate(((li_f_ref, lf_f_ref), (li_b_ref, lf_b_ref))):
        li = li_ref[...] + bias_ref[d * M_HEADS + head]
        fpre = lf_ref[...] + bias_ref[2 * M_HEADS + d * M_HEADS + head]
        lf = jnp.minimum(fpre, 0.0) - jnp.log(1.0 + jnp.exp(-jnp.abs(fpre)))
        b = _lane_cumsum(lf, reverse=(d == 1))
        u = li - b
        tot = jnp.broadcast_to(jnp.sum(lf, axis=-1, keepdims=True), lf.shape)
        umax = jnp.broadcast_to(jnp.max(u, axis=-1, keepdims=True), lf.shape)
        gl_ref[d, 0] = u
        gl_ref[d, 1] = lf
        gl_ref[d, 2] = tot
        gl_ref[d, 3] = umax

    cst_ref[...] = jnp.zeros(cst_ref.shape, F32)

    row = lax.broadcasted_iota(jnp.int32, (CHUNK, CHUNK), 0)
    col = lax.broadcasted_iota(jnp.int32, (CHUNK, CHUNK), 1)
    masks = (col <= row, col >= row)

    def chunk_step(c, m_prev, d, h_ref):
        mask = masks[d]
        r0 = pl.multiple_of(c * CHUNK, CHUNK)
        q = qc_ref[pl.ds(r0, CHUNK), :]
        k = kc_ref[pl.ds(r0, CHUNK), :]
        va = va_ref[pl.ds(r0, CHUNK), :]
        u = gl_ref[d, 0, pl.ds(c, 1), :]
        lf = gl_ref[d, 1, pl.ds(c, 1), :]
        tot = gl_ref[d, 2, pl.ds(c, 1), :]
        umax = gl_ref[d, 3, pl.ds(c, 1), :]

        s = lax.dot_general(q, k.astype(BF16), (((1,), (1,)), ((), ())), preferred_element_type=F32)
        um = jnp.where(mask, u, NEG_INF)
        mloc = jnp.maximum(jnp.max(um, axis=-1, keepdims=True), m_prev)
        dt = jnp.exp(um - mloc)
        inter = jnp.exp(m_prev - mloc)
        lhs = jnp.concatenate([(s * dt).astype(BF16), (q.astype(F32) * inter).astype(BF16)], axis=1)
        state = cst_ref[d]
        rhs = jnp.concatenate([va, state.astype(BF16)], axis=0)
        out = jnp.dot(lhs, rhs, preferred_element_type=F32)
        b_col = jnp.sum(jnp.where(mask, lf, 0.0), axis=-1, keepdims=True)
        stab = jnp.exp(-(b_col + mloc))
        h_ref[pl.ds(r0, CHUNK), :] = out[:, 0:LANES] / jnp.maximum(jnp.abs(out[:, LANES:]), stab)

        m_new = jnp.maximum(tot + m_prev, umax + tot)
        decay = jnp.exp(tot + m_prev - m_new)
        w = jnp.exp(u + tot - m_new)
        ktw = (k.T * w).astype(BF16)
        upd = jnp.dot(ktw, va, preferred_element_type=F32)
        cst_ref[d] = jnp.concatenate([decay, decay], axis=1) * state + upd
        return m_new

    def scan_body(i, carry):
        m_f, m_b = carry
        m_f = chunk_step(i, m_f, 0, hf_ref)
        m_b = chunk_step(n_chunks - 1 - i, m_b, 1, hb_ref)
        return m_f, m_b

    m0 = jnp.full((1, LANES), NEG_INF, F32)
    lax.fori_loop(0, n_chunks, scan_body, (m0, m0))

    def fin_body(c, carry):
        r0 = pl.multiple_of(c * CHUNK, CHUNK)
        hm = hf_ref[pl.ds(r0, CHUNK), :] + hb_ref[pl.ds(r0, CHUNK), :]
        ms = jnp.mean(hm * hm, axis=-1, keepdims=True)
        y = hm * lax.rsqrt(ms + NORM_EPS) * hg_ref[...] * _sigmoid(o_ref[pl.ds(r0, CHUNK), :])
        y_ref[pl.ds(r0, CHUNK), :] = y.astype(y_ref.dtype)
        return carry

    lax.fori_loop(0, n_chunks, fin_body, 0)


def _mlstm(proj3, gates_row, gate_bias, conv_w, head_norm_g):
    batch, seq, _ = proj3.shape
    n_chunks = seq // CHUNK
    col = lambda off: pl.BlockSpec((None, seq, LANES), lambda b, h: (b, 0, off + h))
    gate = lambda off: pl.BlockSpec((None, None, n_chunks, LANES), lambda b, h: (b, off + h, 0, 0))
    return pl.pallas_call(
        _mlstm_kernel,
        grid=(batch, M_HEADS),
        in_specs=[
            pl.BlockSpec(memory_space=pltpu.SMEM),
            col(0), col(M_HEADS), col(2 * M_HEADS), col(3 * M_HEADS),
            gate(0), gate(M_HEADS), gate(2 * M_HEADS), gate(3 * M_HEADS),
            pl.BlockSpec((CONV_WIDTH, LANES), lambda b, h: (0, h)),
            pl.BlockSpec((CONV_WIDTH, LANES), lambda b, h: (0, M_HEADS + h)),
            pl.BlockSpec((1, LANES), lambda b, h: (0, h)),
        ],
        out_specs=pl.BlockSpec((None, seq, LANES), lambda b, h: (b, 0, h)),
        out_shape=jax.ShapeDtypeStruct((batch, seq, M_WIDTH), BF16),
        scratch_shapes=[
            pltpu.VMEM((seq + 16, LANES), F32),
            pltpu.VMEM((seq, LANES), BF16),
            pltpu.VMEM((seq, LANES), F32),
            pltpu.VMEM((seq, 2 * LANES), BF16),
            pltpu.VMEM((seq, LANES), F32),
            pltpu.VMEM((seq, LANES), F32),
            pltpu.VMEM((2, M_HEAD_DIM, 2 * LANES), F32),
            pltpu.VMEM((2, 4, n_chunks, LANES), F32),
        ],
        compiler_params=pltpu.CompilerParams(
            dimension_semantics=("arbitrary", "arbitrary"), vmem_limit_bytes=VMEM_LIMIT),
        name="mlstm",
    )(gate_bias, proj3, proj3, proj3, proj3, gates_row, gates_row, gates_row, gates_row,
      conv_w, conv_w, head_norm_g)


LOG2E = 1.4426950408889634
ATTN_UNROLL = 8


def _attn_kernel(q_ref, k_ref, v_ref, cos_ref, sin_ref, bias_ref, bias8_ref, y_ref,
                 qr_ref, kr_ref, c4_ref, c8_ref, num_ref, den_ref, mx_ref):
    seq = q_ref.shape[0]
    rope_rows = 256

    lane_r = lax.broadcasted_iota(jnp.int32, (rope_rows, LANES), 1)
    first_half = (lane_r % A_HEAD_DIM) < (A_HEAD_DIM // 2)

    def rope_body(c, carry):
        r0 = pl.multiple_of(c * rope_rows, rope_rows)
        rows = pl.ds(r0, rope_rows)
        cos = cos_ref[rows, :]
        sin = sin_ref[rows, :]
        for src, dst, scale in ((q_ref, qr_ref, LOG2E / math.sqrt(A_HEAD_DIM)), (k_ref, kr_ref, 1.0)):
            x = src[rows, :]
            partner = jnp.where(first_half, pltpu.roll(x, LANES - A_HEAD_DIM // 2, axis=1),
                                pltpu.roll(x, A_HEAD_DIM // 2, axis=1))
            out = x * cos + partner * sin
            if scale != 1.0:
                out = out * scale
            dst[rows, :] = out
        return carry

    lax.fori_loop(0, seq // rope_rows, rope_body, 0)

    d4, d16 = DILATIONS[1], DILATIONS[2]
    d8 = d16 // 2
    len4, len8 = seq // d4, seq // d8

    def split4_body(i, carry):
        r = i // (len4 // rope_rows)
        l0 = (i % (len4 // rope_rows)) * rope_rows
        dst = pl.ds(pl.multiple_of(r * len4 + l0, rope_rows), rope_rows)
        for t, src in enumerate((qr_ref, kr_ref, v_ref)):
            c4_ref[t, dst, :] = src[pl.ds(r + d4 * l0, rope_rows, stride=d4), :]
        return carry

    lax.fori_loop(0, seq // rope_rows, split4_body, 0)

    def split8_body(r, carry):
        src_rows = pl.ds((r % d4) * len4 + r // d4, len8, stride=d8 // d4)
        dst = pl.ds(pl.multiple_of(r * len8, len8), len8)
        for t in range(3):
            c8_ref[t, dst, :] = c4_ref[t, src_rows, :]
        return carry

    lax.fori_loop(0, d8, split8_body, 0)

    lane_q = lax.broadcasted_iota(jnp.int32, (QB, LANES), 1)
    head0_q = lane_q < A_HEAD_DIM

    def block(q, k, v, bias):
        span = k.shape[0]
        q2 = jnp.concatenate([jnp.where(head0_q, q, 0.0), jnp.where(head0_q, 0.0, q)], axis=0).astype(BF16)
        s = lax.dot_general(q2, k.astype(BF16), (((1,), (1,)), ((), ())), preferred_element_type=F32)
        s = s + jnp.concatenate([bias, bias], axis=0)
        m = jnp.max(s, axis=-1, keepdims=True)
        p = jnp.exp2(s - m).astype(BF16)
        rhs = jnp.concatenate([v.astype(BF16), jnp.ones((span, LANES), BF16)], axis=1)
        acc0 = jnp.dot(p[0:QB], rhs, preferred_element_type=F32)
        acc1 = jnp.dot(p[QB:2 * QB], rhs, preferred_element_type=F32)
        return (jnp.where(head0_q, acc0[:, 0:LANES], acc1[:, 0:LANES]),
                jnp.where(head0_q, acc0[:, LANES:], acc1[:, LANES:]),
                jnp.where(head0_q, m[0:QB], m[QB:2 * QB]))

    stats_refs = (num_ref, den_ref, mx_ref)

    def band_body(i, carry, pat, mod, half, srcs, biases):
        length = seq // mod
        n_blocks = length // QB
        r = i // n_blocks
        j = i % n_blocks
        q0 = j * QB
        k0 = jnp.clip(q0 - half, 0, length - SPAN)
        variant = (q0 - k0) // half
        qrows = pl.ds(pl.multiple_of(r * length + q0, QB), QB)
        krows = pl.ds(pl.multiple_of(r * length + k0, HALF_WINDOW), SPAN)
        if mod == 1:
            out_rows = qrows
        else:
            out_rows = pl.ds(r + mod * q0, QB, stride=mod)
        res = block(srcs[0](qrows), srcs[1](krows), srcs[2](krows), biases[variant])
        for ref, val in zip(stats_refs, res):
            ref[pat, out_rows, :] = val
        return carry

    nat = (lambda rows: qr_ref[rows, :], lambda rows: kr_ref[rows, :], lambda rows: v_ref[rows, :])
    cm4 = tuple((lambda rows, t=t: c4_ref[t, rows, :]) for t in range(3))
    cm8 = tuple((lambda rows, t=t: c8_ref[t, rows, :]) for t in range(3))
    for pat, (mod, half, srcs, biases) in enumerate((
            (DILATIONS[0], HALF_WINDOW, nat, bias_ref),
            (d4, HALF_WINDOW, cm4, bias_ref),
            (d8, HALF_WINDOW * (d16 // d8), cm8, bias8_ref))):
        lax.fori_loop(0, seq // QB,
                      functools.partial(band_body, pat=pat, mod=mod, half=half, srcs=srcs, biases=biases),
                      0, unroll=ATTN_UNROLL)

    def merge_body(c, carry):
        rows = pl.ds(pl.multiple_of(c * rope_rows, rope_rows), rope_rows)
        m_all = jnp.maximum(jnp.maximum(mx_ref[0, rows, :], mx_ref[1, rows, :]), mx_ref[2, rows, :])
        num = jnp.zeros((rope_rows, LANES), F32)
        den = jnp.zeros((rope_rows, LANES), F32)
        for pat in range(len(DILATIONS)):
            w = jnp.exp2(mx_ref[pat, rows, :] - m_all)
            num = num + w * num_ref[pat, rows, :]
            den = den + w * den_ref[pat, rows, :]
        y_ref[rows, :] = (num / den).astype(y_ref.dtype)
        return carry

    lax.fori_loop(0, seq // rope_rows, merge_body, 0)


def _attention(proj3, cos_tab, sin_tab, bias, bias8):
    batch, seq, _ = proj3.shape
    assert DILATIONS == (1, 4, 16) and seq * 2 // DILATIONS[2] == SPAN
    n_pairs = A_WIDTH // LANES
    base = 4 * M_WIDTH // LANES
    col = lambda off: pl.BlockSpec((None, seq, LANES), lambda b, p: (b, 0, base + off + p))
    const2 = lambda shape: pl.BlockSpec(shape, lambda b, p: (0,) * len(shape))
    n_pat = len(DILATIONS)
    return pl.pallas_call(
        _attn_kernel,
        grid=(batch, n_pairs),
        in_specs=[
            col(0), col(n_pairs), col(2 * n_pairs),
            const2((seq, LANES)), const2((seq, LANES)),
            const2((3, QB, SPAN)), const2((2, QB, SPAN)),
        ],
        out_specs=pl.BlockSpec((None, seq, LANES), lambda b, p: (b, 0, p)),
        out_shape=jax.ShapeDtypeStruct((batch, seq, A_WIDTH), BF16),
        scratch_shapes=[
            pltpu.VMEM((seq, LANES), F32),
            pltpu.VMEM((seq, LANES), F32),
            pltpu.VMEM((3, seq, LANES), F32),
            pltpu.VMEM((3, seq, LANES), F32),
            pltpu.VMEM((n_pat, seq, LANES), F32),
            pltpu.VMEM((n_pat, seq, LANES), F32),
            pltpu.VMEM((n_pat, seq, LANES), F32),
        ],
        compiler_params=pltpu.CompilerParams(
            dimension_semantics=("arbitrary", "arbitrary"), vmem_limit_bytes=VMEM_LIMIT),
        name="dilated_attn",
    )(proj3, proj3, proj3, cos_tab, sin_tab, bias, bias8)


def _out_mlp_kernel(x_ref, ym_ref, ya_ref, wo_ref, g2_ref, wu_ref, wd_ref, fg_ref, o_ref, usq_ref,
                    *, ff_chunk, final):
    x1 = (x_ref[...]
          + jnp.dot(ym_ref[...], wo_ref[0:M_WIDTH, :], preferred_element_type=F32)
          + jnp.dot(ya_ref[...], wo_ref[M_WIDTH:, :], preferred_element_type=F32))
    ms = jnp.mean(x1 * x1, axis=-1, keepdims=True)
    h = (x1 * lax.rsqrt(ms + NORM_EPS) * g2_ref[...]).astype(BF16)
    for c in range(0, D_FF, ff_chunk):
        u = jnp.maximum(jnp.dot(h, wu_ref[:, c:c + ff_chunk], preferred_element_type=F32), 0.0)
        usq_ref[:, c:c + ff_chunk] = (u * u).astype(BF16)
    out = x1 + jnp.dot(usq_ref[...], wd_ref[...], preferred_element_type=F32)
    if final:
        ms = jnp.mean(out * out, axis=-1, keepdims=True)
        out = out * lax.rsqrt(ms + NORM_EPS) * fg_ref[...]
    o_ref[...] = out


def _out_mlp(x2d, ym, ya, w_out, g2, w_up, w_down, final_g, final, tm=512):
    m = x2d.shape[0]
    resident = lambda shape: pl.BlockSpec(shape, lambda i: (0, 0), pipeline_mode=pl.Buffered(1))
    return pl.pallas_call(
        functools.partial(_out_mlp_kernel, ff_chunk=1024, final=final),
        grid=(m // tm,),
        in_specs=[
            pl.BlockSpec((tm, D_MODEL), lambda i: (i, 0)),
            pl.BlockSpec((tm, M_WIDTH), lambda i: (i, 0)),
            pl.BlockSpec((tm, A_WIDTH), lambda i: (i, 0)),
            resident((D_MODEL, D_MODEL)),
            resident((1, D_MODEL)),
            resident((D_MODEL, D_FF)),
            resident((D_FF, D_MODEL)),
            resident((1, D_MODEL)),
        ],
        out_specs=pl.BlockSpec((tm, D_MODEL), lambda i: (i, 0)),
        out_shape=jax.ShapeDtypeStruct((m, D_MODEL), F32),
        scratch_shapes=[pltpu.VMEM((tm, D_FF), BF16)],
        compiler_params=pltpu.CompilerParams(
            dimension_semantics=("arbitrary",), vmem_limit_bytes=VMEM_LIMIT),
        name="out_mlp",
    )(x2d, ym, ya, w_out, g2, w_up, w_down, final_g)


def _rope_tables(seq):
    half = A_HEAD_DIM // 2
    inv_freq = ROPE_THETA ** (-jnp.arange(half, dtype=F32) / half)
    ang = jnp.arange(seq, dtype=F32)[:, None] * inv_freq[None, :]
    cos = jnp.tile(jnp.cos(ang), (1, LANES // half))
    sin = jnp.sin(ang)
    sin = jnp.tile(jnp.concatenate([-sin, sin], axis=1), (1, LANES // A_HEAD_DIM))
    return cos, sin


def _band_bias():
    row = jnp.arange(QB)[:, None]
    col = jnp.arange(SPAN)[None, :]
    variants = []
    for off in (0, HALF_WINDOW, 2 * HALF_WINDOW):
        rel = col - off - row
        variants.append(jnp.where(jnp.abs(rel) <= HALF_WINDOW, 0.0, NEG_INF))
    variants8 = []
    for off in (0, QB):
        rel = col - off - row
        variants8.append(jnp.where((jnp.abs(rel) <= 2 * HALF_WINDOW) & (rel % 2 == 0), 0.0, NEG_INF))
    return jnp.stack(variants).astype(F32), jnp.stack(variants8).astype(F32)


def kernel(x, norm1_g, w_in, conv_w, gate_i_b, gate_f_b, head_norm_g, w_out, norm2_g, w_up, w_down, final_g):
    batch, seq, _ = x.shape
    depth = w_in.shape[0]
    n_chunks = seq // CHUNK
    cos_tab, sin_tab = _rope_tables(seq)
    bias, bias8 = _band_bias()
    gate_lo = 4 * M_WIDTH

    x2d = x.reshape(batch * seq, D_MODEL)
    for layer in range(depth):
        wl = w_in[layer]
        w_cat = jnp.concatenate(
            [wl[:, :gate_lo], wl[:, gate_lo + N_GATES:], wl[:, gate_lo:gate_lo + N_GATES],
             jnp.zeros((D_MODEL, LANES - N_GATES), F32)], axis=1).astype(BF16)
        proj = _in_proj(x2d, norm1_g[layer][None, :], w_cat)
        proj3 = proj.reshape(batch, seq, PROJ_COLS)
        gates_row = jnp.transpose(proj3[:, :, MAIN_COLS:MAIN_COLS + N_GATES], (0, 2, 1)).reshape(
            batch, N_GATES, n_chunks, CHUNK)
        gate_bias = jnp.concatenate([gate_i_b[layer], gate_f_b[layer]]).astype(F32)
        ym = _mlstm(proj3, gates_row, gate_bias, conv_w[layer], head_norm_g[layer][None, :])
        ya = _attention(proj3, cos_tab, sin_tab, bias, bias8)
        x2d = _out_mlp(x2d, ym.reshape(batch * seq, M_WIDTH), ya.reshape(batch * seq, A_WIDTH),
                       w_out[layer].astype(BF16), norm2_g[layer][None, :],
                       w_up[layer].astype(BF16), w_down[layer].astype(BF16),
                       final_g[None, :], final=(layer == depth - 1))
    return x2d.reshape(batch, seq, D_MODEL)
```

```python
import functools
import math

import jax
import jax.numpy as jnp
from jax import lax
from jax.experimental import pallas as pl
from jax.experimental.pallas import tpu as pltpu

F32 = jnp.float32
BF16 = jnp.bfloat16

D_MODEL = 1024
M_HEADS = 4
M_HEAD_DIM = 128
M_WIDTH = M_HEADS * M_HEAD_DIM
A_HEADS = 8
A_HEAD_DIM = 64
A_WIDTH = A_HEADS * A_HEAD_DIM
D_FF = 4 * D_MODEL
CONV_WIDTH = 5
ROPE_THETA = 10000.0
NORM_EPS = 1e-6
NEG_INF = -1e30
DILATIONS = (1, 4, 16)
HALF_WINDOW = 64

LANES = 128
N_GATES = 4 * M_HEADS
MAIN_COLS = 4 * M_WIDTH + 3 * A_WIDTH
PROJ_COLS = MAIN_COLS + LANES
CHUNK = 128
QB = 128
SPAN = QB + 2 * HALF_WINDOW
VMEM_LIMIT = 56 * 1024 * 1024
LOG2E = 1.4426950408889634
N_GATE_ROWS = 8
STATE_UNROLL = 4
OUT_UNROLL = 2


def _sigmoid(x):
    return 1.0 / (1.0 + jnp.exp(-x))


def _in_proj_kernel(x_ref, g_ref, w_ref, o_ref, *, col_chunk):
    x = x_ref[...]
    ms = jnp.mean(x * x, axis=-1, keepdims=True)
    h = (x * lax.rsqrt(ms + NORM_EPS) * g_ref[...]).astype(BF16)
    n = w_ref.shape[1]
    for c in range(0, n, col_chunk):
        w = min(col_chunk, n - c)
        o_ref[:, c:c + w] = jnp.dot(h, w_ref[:, c:c + w], preferred_element_type=F32)


def _in_proj(x2d, g, w, tm=512):
    m = x2d.shape[0]
    n = w.shape[1]
    return pl.pallas_call(
        functools.partial(_in_proj_kernel, col_chunk=512),
        grid=(m // tm,),
        in_specs=[
            pl.BlockSpec((tm, D_MODEL), lambda i: (i, 0)),
            pl.BlockSpec((1, D_MODEL), lambda i: (0, 0)),
            pl.BlockSpec((D_MODEL, n), lambda i: (0, 0), pipeline_mode=pl.Buffered(1)),
        ],
        out_specs=pl.BlockSpec((tm, n), lambda i: (i, 0)),
        out_shape=jax.ShapeDtypeStruct((m, n), F32),
        compiler_params=pltpu.CompilerParams(
            dimension_semantics=("arbitrary",), vmem_limit_bytes=VMEM_LIMIT),
        name="in_proj",
    )(x2d, g, w)


def _lane_cumsum(x, reverse):
    lane = lax.broadcasted_iota(jnp.int32, x.shape, 1)
    k = 1
    while k < LANES:
        if reverse:
            x = x + jnp.where(lane < LANES - k, pltpu.roll(x, LANES - k, axis=1), 0.0)
        else:
            x = x + jnp.where(lane >= k, pltpu.roll(x, k, axis=1), 0.0)
        k *= 2
    return x


def _mlstm_kernel(bias_ref, q_ref, k_ref, v_ref, o_ref, li_f_ref, li_b_ref, lf_f_ref, lf_b_ref,
                  cwq_ref, cwk_ref, hg_ref, y_ref,
                  pad_ref, qc_ref, kc_ref, va_ref, cst_ref, cprev_ref, gl_ref):
    seq = q_ref.shape[0]
    n_chunks = seq // CHUNK
    head = pl.program_id(1)

    pad_ref[0:8, :] = jnp.zeros((8, LANES), F32)
    pad_ref[8 + seq:16 + seq, :] = jnp.zeros((8, LANES), F32)

    def conv_into(src_ref, cw_ref, dst_ref, scale):
        pad_ref[8:8 + seq, :] = src_ref[...]

        def body(c, carry):
            r0 = pl.multiple_of(c * CHUNK, CHUNK)
            acc = jnp.zeros((CHUNK, LANES), F32)
            for t in range(CONV_WIDTH):
                acc = acc + cw_ref[t:t + 1, :] * pad_ref[pl.ds(r0 + 8 + t - CONV_WIDTH // 2, CHUNK), :]
            act = acc * _sigmoid(acc)
            if scale != 1.0:
                act = act * scale
            dst_ref[pl.ds(r0, CHUNK), :] = act.astype(dst_ref.dtype)
            return carry

        lax.fori_loop(0, n_chunks, body, 0)

    conv_into(q_ref, cwq_ref, qc_ref, 1.0)
    conv_into(k_ref, cwk_ref, kc_ref, 1.0 / math.sqrt(M_HEAD_DIM))

    va_ref[:, 0:LANES] = v_ref[...].astype(BF16)
    va_ref[:, LANES:2 * LANES] = jnp.ones((seq, LANES), BF16)

    U, LF, TOT, GMAX, MPREV, MNEW, WROW, DECAY = range(N_GATE_ROWS)
    for d, (li_ref, lf_ref) in enumerate(((li_f_ref, lf_f_ref), (li_b_ref, lf_b_ref))):
        li = (li_ref[...] + bias_ref[d * M_HEADS + head]) * LOG2E
        fpre = lf_ref[...] + bias_ref[2 * M_HEADS + d * M_HEADS + head]
        lf = (jnp.minimum(fpre, 0.0) - jnp.log(1.0 + jnp.exp(-jnp.abs(fpre)))) * LOG2E
        b = _lane_cumsum(lf, reverse=(d == 1))
        u = li - b
        tot = jnp.broadcast_to(jnp.sum(lf, axis=-1, keepdims=True), lf.shape)
        gl_ref[d, U] = u
        gl_ref[d, LF] = lf
        gl_ref[d, TOT] = tot
        gl_ref[d, GMAX] = jnp.broadcast_to(jnp.max(u, axis=-1, keepdims=True), lf.shape) + tot
        m = jnp.full((1, LANES), NEG_INF, F32)
        for step in range(n_chunks):
            c = step if d == 0 else n_chunks - 1 - step
            gl_ref[d, MPREV, c:c + 1, :] = m
            m_new = jnp.maximum(gl_ref[d, TOT, c:c + 1, :] + m, gl_ref[d, GMAX, c:c + 1, :])
            gl_ref[d, DECAY, c:c + 1, :] = jnp.exp2(gl_ref[d, TOT, c:c + 1, :] + m - m_new)
            gl_ref[d, MNEW, c:c + 1, :] = m_new
            m = m_new
        gl_ref[d, WROW] = jnp.exp2(u + tot - gl_ref[d, MNEW])

    cst_ref[...] = jnp.zeros(cst_ref.shape, F32)

    def state_body(i, carry):
        for d in range(2):
            c = i if d == 0 else n_chunks - 1 - i
            r0 = pl.multiple_of(c * CHUNK, CHUNK)
            k = kc_ref[pl.ds(r0, CHUNK), :]
            va = va_ref[pl.ds(r0, CHUNK), :]
            w = gl_ref[d, WROW, pl.ds(c, 1), :]
            decay = gl_ref[d, DECAY, pl.ds(c, 1), :]
            upd = jnp.dot((k.T * w).astype(BF16), va, preferred_element_type=F32)
            state = cst_ref[d]
            cprev_ref[d, c] = state.astype(BF16)
            cst_ref[d] = jnp.concatenate([decay, decay], axis=1) * state + upd
        return carry

    lax.fori_loop(0, n_chunks, state_body, 0, unroll=STATE_UNROLL)

    row = lax.broadcasted_iota(jnp.int32, (CHUNK, CHUNK), 0)
    col = lax.broadcasted_iota(jnp.int32, (CHUNK, CHUNK), 1)
    masks = (col <= row, col >= row)

    def out_body(c, carry):
        r0 = pl.multiple_of(c * CHUNK, CHUNK)
        q = qc_ref[pl.ds(r0, CHUNK), :]
        k = kc_ref[pl.ds(r0, CHUNK), :]
        va = va_ref[pl.ds(r0, CHUNK), :]
        s = lax.dot_general(q, k.astype(BF16), (((1,), (1,)), ((), ())), preferred_element_type=F32)
        qf = q.astype(F32)
        hm = jnp.zeros((CHUNK, LANES), F32)
        for d in range(2):
            mask = masks[d]
            u = gl_ref[d, U, pl.ds(c, 1), :]
            lf = gl_ref[d, LF, pl.ds(c, 1), :]
            m_prev = gl_ref[d, MPREV, pl.ds(c, 1), :]
            um = jnp.where(mask, u, NEG_INF)
            mloc = jnp.maximum(jnp.max(um, axis=-1, keepdims=True), m_prev)
            dt = jnp.exp2(um - mloc)
            inter = jnp.exp2(m_prev - mloc)
            lhs = jnp.concatenate([(s * dt).astype(BF16), (qf * inter).astype(BF16)], axis=1)
            rhs = jnp.concatenate([va, cprev_ref[d, c]], axis=0)
            out = jnp.dot(lhs, rhs, preferred_element_type=F32)
            b_col = jnp.sum(jnp.where(mask, lf, 0.0), axis=-1, keepdims=True)
            stab = jnp.exp2(-(b_col + mloc))
            hm = hm + out[:, 0:LANES] / jnp.maximum(jnp.abs(out[:, LANES:]), stab)
        ms = jnp.mean(hm * hm, axis=-1, keepdims=True)
        y = hm * lax.rsqrt(ms + NORM_EPS) * hg_ref[...] * _sigmoid(o_ref[pl.ds(r0, CHUNK), :])
        y_ref[pl.ds(r0, CHUNK), :] = y.astype(y_ref.dtype)
        return carry

    lax.fori_loop(0, n_chunks, out_body, 0, unroll=OUT_UNROLL)


def _mlstm(proj3, gates_row, gate_bias, conv_w, head_norm_g):
    batch, seq, _ = proj3.shape
    n_chunks = seq // CHUNK
    col = lambda off: pl.BlockSpec((None, seq, LANES), lambda b, h: (b, 0, off + h))
    gate = lambda off: pl.BlockSpec((None, None, n_chunks, LANES), lambda b, h: (b, off + h, 0, 0))
    return pl.pallas_call(
        _mlstm_kernel,
        grid=(batch, M_HEADS),
        in_specs=[
            pl.BlockSpec(memory_space=pltpu.SMEM),
            col(0), col(M_HEADS), col(2 * M_HEADS), col(3 * M_HEADS),
            gate(0), gate(M_HEADS), gate(2 * M_HEADS), gate(3 * M_HEADS),
            pl.BlockSpec((CONV_WIDTH, LANES), lambda b, h: (0, h)),
            pl.BlockSpec((CONV_WIDTH, LANES), lambda b, h: (0, M_HEADS + h)),
            pl.BlockSpec((1, LANES), lambda b, h: (0, h)),
        ],
        out_specs=pl.BlockSpec((None, seq, LANES), lambda b, h: (b, 0, h)),
        out_shape=jax.ShapeDtypeStruct((batch, seq, M_WIDTH), BF16),
        scratch_shapes=[
            pltpu.VMEM((seq + 16, LANES), F32),
            pltpu.VMEM((seq, LANES), BF16),
            pltpu.VMEM((seq, LANES), F32),
            pltpu.VMEM((seq, 2 * LANES), BF16),
            pltpu.VMEM((2, M_HEAD_DIM, 2 * LANES), F32),
            pltpu.VMEM((2, n_chunks, M_HEAD_DIM, 2 * LANES), BF16),
            pltpu.VMEM((2, N_GATE_ROWS, n_chunks, LANES), F32),
        ],
        compiler_params=pltpu.CompilerParams(
            dimension_semantics=("arbitrary", "arbitrary"), vmem_limit_bytes=VMEM_LIMIT),
        name="mlstm",
    )(gate_bias, proj3, proj3, proj3, proj3, gates_row, gates_row, gates_row, gates_row,
      conv_w, conv_w, head_norm_g)


ATTN_UNROLL = 8


def _attn_kernel(q_ref, k_ref, v_ref, cos_ref, sin_ref, bias_ref, bias8_ref, y_ref,
                 qr_ref, kr_ref, c4_ref, c8_ref, num_ref, den_ref, mx_ref):
    seq = q_ref.shape[0]
    rope_rows = 256

    lane_r = lax.broadcasted_iota(jnp.int32, (rope_rows, LANES), 1)
    first_half = (lane_r % A_HEAD_DIM) < (A_HEAD_DIM // 2)

    def rope_body(c, carry):
        r0 = pl.multiple_of(c * rope_rows, rope_rows)
        rows = pl.ds(r0, rope_rows)
        cos = cos_ref[rows, :]
        sin = sin_ref[rows, :]
        for src, dst, scale in ((q_ref, qr_ref, LOG2E / math.sqrt(A_HEAD_DIM)), (k_ref, kr_ref, 1.0)):
            x = src[rows, :]
            partner = jnp.where(first_half, pltpu.roll(x, LANES - A_HEAD_DIM // 2, axis=1),
                                pltpu.roll(x, A_HEAD_DIM // 2, axis=1))
            out = x * cos + partner * sin
            if scale != 1.0:
                out = out * scale
            dst[rows, :] = out
        return carry

    lax.fori_loop(0, seq // rope_rows, rope_body, 0)

    d4, d16 = DILATIONS[1], DILATIONS[2]
    d8 = d16 // 2
    len4, len8 = seq // d4, seq // d8

    def split4_body(i, carry):
        r = i // (len4 // rope_rows)
        l0 = (i % (len4 // rope_rows)) * rope_rows
        dst = pl.ds(pl.multiple_of(r * len4 + l0, rope_rows), rope_rows)
        for t, src in enumerate((qr_ref, kr_ref, v_ref)):
            c4_ref[t, dst, :] = src[pl.ds(r + d4 * l0, rope_rows, stride=d4), :]
        return carry

    lax.fori_loop(0, seq // rope_rows, split4_body, 0)

    def split8_body(r, carry):
        src_rows = pl.ds((r % d4) * len4 + r // d4, len8, stride=d8 // d4)
        dst = pl.ds(pl.multiple_of(r * len8, len8), len8)
        for t in range(3):
            c8_ref[t, dst, :] = c4_ref[t, src_rows, :]
        return carry

    lax.fori_loop(0, d8, split8_body, 0)

    lane_q = lax.broadcasted_iota(jnp.int32, (QB, LANES), 1)
    head0_q = lane_q < A_HEAD_DIM

    def block(q, k, v, bias):
        span = k.shape[0]
        q2 = jnp.concatenate([jnp.where(head0_q, q, 0.0), jnp.where(head0_q, 0.0, q)], axis=0).astype(BF16)
        s = lax.dot_general(q2, k.astype(BF16), (((1,), (1,)), ((), ())), preferred_element_type=F32)
        s = s + jnp.concatenate([bias, bias], axis=0)
        m = jnp.max(s, axis=-1, keepdims=True)
        p = jnp.exp2(s - m).astype(BF16)
        rhs = jnp.concatenate([v.astype(BF16), jnp.ones((span, LANES), BF16)], axis=1)
        acc0 = jnp.dot(p[0:QB], rhs, preferred_element_type=F32)
        acc1 = jnp.dot(p[QB:2 * QB], rhs, preferred_element_type=F32)
        return (jnp.where(head0_q, acc0[:, 0:LANES], acc1[:, 0:LANES]),
                jnp.where(head0_q, acc0[:, LANES:], acc1[:, LANES:]),
                jnp.where(head0_q, m[0:QB], m[QB:2 * QB]))

    stats_refs = (num_ref, den_ref, mx_ref)

    def band_body(i, carry, pat, mod, half, srcs, biases):
        length = seq // mod
        n_blocks = length // QB
        r = i // n_blocks
        j = i % n_blocks
        q0 = j * QB
        k0 = jnp.clip(q0 - half, 0, length - SPAN)
        variant = (q0 - k0) // half
        qrows = pl.ds(pl.multiple_of(r * length + q0, QB), QB)
        krows = pl.ds(pl.multiple_of(r * length + k0, HALF_WINDOW), SPAN)
        if mod == 1:
            out_rows = qrows
        else:
            out_rows = pl.ds(r + mod * q0, QB, stride=mod)
        res = block(srcs[0](qrows), srcs[1](krows), srcs[2](krows), biases[variant])
        for ref, val in zip(stats_refs, res):
            ref[pat, out_rows, :] = val
        return carry

    nat = (lambda rows: qr_ref[rows, :], lambda rows: kr_ref[rows, :], lambda rows: v_ref[rows, :])
    cm4 = tuple((lambda rows, t=t: c4_ref[t, rows, :]) for t in range(3))
    cm8 = tuple((lambda rows, t=t: c8_ref[t, rows, :]) for t in range(3))
    for pat, (mod, half, srcs, biases) in enumerate((
            (DILATIONS[0], HALF_WINDOW, nat, bias_ref),
            (d4, HALF_WINDOW, cm4, bias_ref),
            (d8, HALF_WINDOW * (d16 // d8), cm8, bias8_ref))):
        lax.fori_loop(0, seq // QB,
                      functools.partial(band_body, pat=pat, mod=mod, half=half, srcs=srcs, biases=biases),
                      0, unroll=ATTN_UNROLL)

    def merge_body(c, carry):
        rows = pl.ds(pl.multiple_of(c * rope_rows, rope_rows), rope_rows)
        m_all = jnp.maximum(jnp.maximum(mx_ref[0, rows, :], mx_ref[1, rows, :]), mx_ref[2, rows, :])
        num = jnp.zeros((rope_rows, LANES), F32)
        den = jnp.zeros((rope_rows, LANES), F32)
        for pat in range(len(DILATIONS)):
            w = jnp.exp2(mx_ref[pat, rows, :] - m_all)
            num = num + w * num_ref[pat, rows, :]
            den = den + w * den_ref[pat, rows, :]
        y_ref[rows, :] = (num / den).astype(y_ref.dtype)
        return carry

    lax.fori_loop(0, seq // rope_rows, merge_body, 0)


def _attention(proj3, cos_tab, sin_tab, bias, bias8):
    batch, seq, _ = proj3.shape
    assert DILATIONS == (1, 4, 16) and seq * 2 // DILATIONS[2] == SPAN
    n_pairs = A_WIDTH // LANES
    base = 4 * M_WIDTH // LANES
    col = lambda off: pl.BlockSpec((None, seq, LANES), lambda b, p: (b, 0, base + off + p))
    const2 = lambda shape: pl.BlockSpec(shape, lambda b, p: (0,) * len(shape))
    n_pat = len(DILATIONS)
    return pl.pallas_call(
        _attn_kernel,
        grid=(batch, n_pairs),
        in_specs=[
            col(0), col(n_pairs), col(2 * n_pairs),
            const2((seq, LANES)), const2((seq, LANES)),
            const2((3, QB, SPAN)), const2((2, QB, SPAN)),
        ],
        out_specs=pl.BlockSpec((None, seq, LANES), lambda b, p: (b, 0, p)),
        out_shape=jax.ShapeDtypeStruct((batch, seq, A_WIDTH), BF16),
        scratch_shapes=[
            pltpu.VMEM((seq, LANES), F32),
            pltpu.VMEM((seq, LANES), F32),
            pltpu.VMEM((3, seq, LANES), F32),
            pltpu.VMEM((3, seq, LANES), F32),
            pltpu.VMEM((n_pat, seq, LANES), F32),
            pltpu.VMEM((n_pat, seq, LANES), F32),
            pltpu.VMEM((n_pat, seq, LANES), F32),
        ],
        compiler_params=pltpu.CompilerParams(
            dimension_semantics=("arbitrary", "arbitrary"), vmem_limit_bytes=VMEM_LIMIT),
        name="dilated_attn",
    )(proj3, proj3, proj3, cos_tab, sin_tab, bias, bias8)


def _out_mlp_kernel(x_ref, ym_ref, ya_ref, wo_ref, g2_ref, wu_ref, wd_ref, fg_ref, o_ref, usq_ref,
                    *, ff_chunk, final):
    x1 = (x_ref[...]
          + jnp.dot(ym_ref[...], wo_ref[0:M_WIDTH, :], preferred_element_type=F32)
          + jnp.dot(ya_ref[...], wo_ref[M_WIDTH:, :], preferred_element_type=F32))
    ms = jnp.mean(x1 * x1, axis=-1, keepdims=True)
    h = (x1 * lax.rsqrt(ms + NORM_EPS) * g2_ref[...]).astype(BF16)
    for c in range(0, D_FF, ff_chunk):
        u = jnp.maximum(jnp.dot(h, wu_ref[:, c:c + ff_chunk], preferred_element_type=F32), 0.0)
        usq_ref[:, c:c + ff_chunk] = (u * u).astype(BF16)
    out = x1 + jnp.dot(usq_ref[...], wd_ref[...], preferred_element_type=F32)
    if final:
        ms = jnp.mean(out * out, axis=-1, keepdims=True)
        out = out * lax.rsqrt(ms + NORM_EPS) * fg_ref[...]
    o_ref[...] = out


def _out_mlp(x2d, ym, ya, w_out, g2, w_up, w_down, final_g, final, tm=512):
    m = x2d.shape[0]
    resident = lambda shape: pl.BlockSpec(shape, lambda i: (0, 0), pipeline_mode=pl.Buffered(1))
    return pl.pallas_call(
        functools.partial(_out_mlp_kernel, ff_chunk=1024, final=final),
        grid=(m // tm,),
        in_specs=[
            pl.BlockSpec((tm, D_MODEL), lambda i: (i, 0)),
            pl.BlockSpec((tm, M_WIDTH), lambda i: (i, 0)),
            pl.BlockSpec((tm, A_WIDTH), lambda i: (i, 0)),
            resident((D_MODEL, D_MODEL)),
            resident((1, D_MODEL)),
            resident((D_MODEL, D_FF)),
            resident((D_FF, D_MODEL)),
            resident((1, D_MODEL)),
        ],
        out_specs=pl.BlockSpec((tm, D_MODEL), lambda i: (i, 0)),
        out_shape=jax.ShapeDtypeStruct((m, D_MODEL), F32),
        scratch_shapes=[pltpu.VMEM((tm, D_FF), BF16)],
        compiler_params=pltpu.CompilerParams(
            dimension_semantics=("arbitrary",), vmem_limit_bytes=VMEM_LIMIT),
        name="out_mlp",
    )(x2d, ym, ya, w_out, g2, w_up, w_down, final_g)


def _rope_tables(seq):
    half = A_HEAD_DIM // 2
    inv_freq = ROPE_THETA ** (-jnp.arange(half, dtype=F32) / half)
    ang = jnp.arange(seq, dtype=F32)[:, None] * inv_freq[None, :]
    cos = jnp.tile(jnp.cos(ang), (1, LANES // half))
    sin = jnp.sin(ang)
    sin = jnp.tile(jnp.concatenate([-sin, sin], axis=1), (1, LANES // A_HEAD_DIM))
    return cos, sin


def _band_bias():
    row = jnp.arange(QB)[:, None]
    col = jnp.arange(SPAN)[None, :]
    variants = []
    for off in (0, HALF_WINDOW, 2 * HALF_WINDOW):
        rel = col - off - row
        variants.append(jnp.where(jnp.abs(rel) <= HALF_WINDOW, 0.0, NEG_INF))
    variants8 = []
    for off in (0, QB):
        rel = col - off - row
        variants8.append(jnp.where((jnp.abs(rel) <= 2 * HALF_WINDOW) & (rel % 2 == 0), 0.0, NEG_INF))
    return jnp.stack(variants).astype(F32), jnp.stack(variants8).astype(F32)


def kernel(x, norm1_g, w_in, conv_w, gate_i_b, gate_f_b, head_norm_g, w_out, norm2_g, w_up, w_down, final_g):
    batch, seq, _ = x.shape
    depth = w_in.shape[0]
    n_chunks = seq // CHUNK
    cos_tab, sin_tab = _rope_tables(seq)
    bias, bias8 = _band_bias()
    gate_lo = 4 * M_WIDTH

    x2d = x.reshape(batch * seq, D_MODEL)
    for layer in range(depth):
        wl = w_in[layer]
        w_cat = jnp.concatenate(
            [wl[:, :gate_lo], wl[:, gate_lo + N_GATES:], wl[:, gate_lo:gate_lo + N_GATES],
             jnp.zeros((D_MODEL, LANES - N_GATES), F32)], axis=1).astype(BF16)
        proj = _in_proj(x2d, norm1_g[layer][None, :], w_cat)
        proj3 = proj.reshape(batch, seq, PROJ_COLS)
        gates_row = jnp.transpose(proj3[:, :, MAIN_COLS:MAIN_COLS + N_GATES], (0, 2, 1)).reshape(
            batch, N_GATES, n_chunks, CHUNK)
        gate_bias = jnp.concatenate([gate_i_b[layer], gate_f_b[layer]]).astype(F32)
        ym = _mlstm(proj3, gates_row, gate_bias, conv_w[layer], head_norm_g[layer][None, :])
        ya = _attention(proj3, cos_tab, sin_tab, bias, bias8)
        x2d = _out_mlp(x2d, ym.reshape(batch * seq, M_WIDTH), ya.reshape(batch * seq, A_WIDTH),
                       w_out[layer].astype(BF16), norm2_g[layer][None, :],
                       w_up[layer].astype(BF16), w_down[layer].astype(BF16),
                       final_g[None, :], final=(layer == depth - 1))
    return x2d.reshape(batch, seq, D_MODEL)
```

```python
import functools
import math

import jax
import jax.numpy as jnp
from jax import lax
from jax.experimental import pallas as pl
from jax.experimental.pallas import tpu as pltpu

F32 = jnp.float32
BF16 = jnp.bfloat16

D_MODEL = 1024
M_HEADS = 4
M_HEAD_DIM = 128
M_WIDTH = M_HEADS * M_HEAD_DIM
A_HEADS = 8
A_HEAD_DIM = 64
A_WIDTH = A_HEADS * A_HEAD_DIM
D_FF = 4 * D_MODEL
CONV_WIDTH = 5
ROPE_THETA = 10000.0
NORM_EPS = 1e-6
NEG_INF = -1e30
DILATIONS = (1, 4, 16)
HALF_WINDOW = 64

LANES = 128
N_GATES = 4 * M_HEADS
MAIN_COLS = 4 * M_WIDTH + 3 * A_WIDTH
PROJ_COLS = MAIN_COLS + LANES
OGATE_COL = 3 * M_WIDTH
AQ_COL = 4 * M_WIDTH
AK_COL = AQ_COL + A_WIDTH
CHUNK = 128
QB = 128
SPAN = QB + 2 * HALF_WINDOW
VMEM_LIMIT = 56 * 1024 * 1024
LOG2E = 1.4426950408889634
N_GATE_ROWS = 8
STATE_UNROLL = 4
OUT_UNROLL = 8


def _sigmoid(x):
    return 1.0 / (1.0 + jnp.exp(-x))


def _in_proj_kernel(x_ref, g_ref, w_ref, rope_ref, o_ref):
    x = x_ref[...]
    ms = jnp.mean(x * x, axis=-1, keepdims=True)
    h = (x * lax.rsqrt(ms + NORM_EPS) * g_ref[...]).astype(BF16)
    for c in range(0, PROJ_COLS, M_WIDTH):
        w = min(M_WIDTH, PROJ_COLS - c)
        r = jnp.dot(h, w_ref[:, c:c + w], preferred_element_type=F32)
        if c == OGATE_COL:
            r = _sigmoid(r)
        if c in (AQ_COL, AK_COL):
            t = 0 if c == AQ_COL else 2
            cos = rope_ref[:, t * LANES:(t + 1) * LANES]
            sin = rope_ref[:, (t + 1) * LANES:(t + 2) * LANES]
            for j in range(0, w, LANES):
                xj = r[:, j:j + LANES]
                o_ref[:, c + j:c + j + LANES] = xj * cos + pltpu.roll(xj, LANES // 2, axis=1) * sin
        else:
            o_ref[:, c:c + w] = r


def _in_proj(x2d, g, w, rope_tab, layer, seq, tm=512):
    m = x2d.shape[0]
    tiles_per_seq = seq // tm
    return pl.pallas_call(
        _in_proj_kernel,
        grid=(m // tm,),
        in_specs=[
            pl.BlockSpec((tm, D_MODEL), lambda i: (i, 0)),
            pl.BlockSpec((None, 1, D_MODEL), lambda i: (layer, 0, 0)),
            pl.BlockSpec((None, D_MODEL, PROJ_COLS), lambda i: (layer, 0, 0), pipeline_mode=pl.Buffered(1)),
            pl.BlockSpec((tm, 4 * LANES), lambda i: (i % tiles_per_seq, 0)),
        ],
        out_specs=pl.BlockSpec((tm, PROJ_COLS), lambda i: (i, 0)),
        out_shape=jax.ShapeDtypeStruct((m, PROJ_COLS), F32),
        compiler_params=pltpu.CompilerParams(
            dimension_semantics=("arbitrary",), vmem_limit_bytes=VMEM_LIMIT),
        name="in_proj",
    )(x2d, g, w, rope_tab)


def _lane_cumsum(x, reverse):
    lane = lax.broadcasted_iota(jnp.int32, x.shape, 1)
    k = 1
    while k < LANES:
        if reverse:
            x = x + jnp.where(lane < LANES - k, pltpu.roll(x, LANES - k, axis=1), 0.0)
        else:
            x = x + jnp.where(lane >= k, pltpu.roll(x, k, axis=1), 0.0)
        k *= 2
    return x


def _mlstm_kernel(bias_ref, q_ref, k_ref, v_ref, o_ref, li_f_ref, li_b_ref, lf_f_ref, lf_b_ref,
                  cwq_ref, cwk_ref, hg_ref, y_ref,
                  pad_ref, qc_ref, kc_ref, va_ref, cst_ref, cprev_ref, gl_ref, *, bias_base):
    seq = q_ref.shape[0]
    n_chunks = seq // CHUNK
    head = pl.program_id(1)

    pad_ref[0:8, :] = jnp.zeros((8, LANES), F32)
    pad_ref[8 + seq:16 + seq, :] = jnp.zeros((8, LANES), F32)

    def conv_into(src_ref, cw_ref, dst_ref, scale):
        pad_ref[8:8 + seq, :] = src_ref[...]

        def body(c, carry):
            r0 = pl.multiple_of(c * CHUNK, CHUNK)
            acc = jnp.zeros((CHUNK, LANES), F32)
            for t in range(CONV_WIDTH):
                acc = acc + cw_ref[t:t + 1, :] * pad_ref[pl.ds(r0 + 8 + t - CONV_WIDTH // 2, CHUNK), :]
            act = acc * _sigmoid(acc)
            if scale != 1.0:
                act = act * scale
            dst_ref[pl.ds(r0, CHUNK), :] = act.astype(dst_ref.dtype)
            return carry

        lax.fori_loop(0, n_chunks, body, 0)

    conv_into(q_ref, cwq_ref, qc_ref, 1.0)
    conv_into(k_ref, cwk_ref, kc_ref, 1.0 / math.sqrt(M_HEAD_DIM))

    va_ref[:, 0:LANES] = v_ref[...].astype(BF16)
    va_ref[:, LANES:2 * LANES] = jnp.ones((seq, LANES), BF16)

    U, LF, TOT, GMAX, MPREV, MNEW, WROW, DECAY = range(N_GATE_ROWS)
    for d, (li_ref, lf_ref) in enumerate(((li_f_ref, lf_f_ref), (li_b_ref, lf_b_ref))):
        li = (li_ref[...] + bias_ref[bias_base + d * M_HEADS + head]) * LOG2E
        fpre = lf_ref[...] + bias_ref[bias_base + 2 * M_HEADS + d * M_HEADS + head]
        lf = (jnp.minimum(fpre, 0.0) - jnp.log(1.0 + jnp.exp(-jnp.abs(fpre)))) * LOG2E
        b = _lane_cumsum(lf, reverse=(d == 1))
        u = li - b
        tot = jnp.broadcast_to(jnp.sum(lf, axis=-1, keepdims=True), lf.shape)
        gl_ref[d, U] = u
        gl_ref[d, LF] = lf
        gl_ref[d, TOT] = tot
        gl_ref[d, GMAX] = jnp.broadcast_to(jnp.max(u, axis=-1, keepdims=True), lf.shape) + tot
        m = jnp.full((1, LANES), NEG_INF, F32)
        for step in range(n_chunks):
            c = step if d == 0 else n_chunks - 1 - step
            gl_ref[d, MPREV, c:c + 1, :] = m
            m_new = jnp.maximum(gl_ref[d, TOT, c:c + 1, :] + m, gl_ref[d, GMAX, c:c + 1, :])
            gl_ref[d, DECAY, c:c + 1, :] = jnp.exp2(gl_ref[d, TOT, c:c + 1, :] + m - m_new)
            gl_ref[d, MNEW, c:c + 1, :] = m_new
            m = m_new
        gl_ref[d, WROW] = jnp.exp2(u + tot - gl_ref[d, MNEW])

    cst_ref[...] = jnp.zeros(cst_ref.shape, F32)

    def state_body(i, carry):
        for d in range(2):
            c = i if d == 0 else n_chunks - 1 - i
            r0 = pl.multiple_of(c * CHUNK, CHUNK)
            k = kc_ref[pl.ds(r0, CHUNK), :]
            va = va_ref[pl.ds(r0, CHUNK), :]
            w = gl_ref[d, WROW, pl.ds(c, 1), :]
            decay = gl_ref[d, DECAY, pl.ds(c, 1), :]
            upd = jnp.dot((k.T * w).astype(BF16), va, preferred_element_type=F32)
            state = cst_ref[d]
            cprev_ref[d, c] = state.astype(BF16)
            cst_ref[d] = jnp.concatenate([decay, decay], axis=1) * state + upd
        return carry

    lax.fori_loop(0, n_chunks, state_body, 0, unroll=STATE_UNROLL)

    row = lax.broadcasted_iota(jnp.int32, (CHUNK, CHUNK), 0)
    col = lax.broadcasted_iota(jnp.int32, (CHUNK, CHUNK), 1)
    masks = (col <= row, col >= row)

    def out_body(c, carry):
        r0 = pl.multiple_of(c * CHUNK, CHUNK)
        q = qc_ref[pl.ds(r0, CHUNK), :]
        k = kc_ref[pl.ds(r0, CHUNK), :]
        va = va_ref[pl.ds(r0, CHUNK), :]
        s = lax.dot_general(q, k.astype(BF16), (((1,), (1,)), ((), ())), preferred_element_type=F32)
        qf = q.astype(F32)
        hm = jnp.zeros((CHUNK, LANES), F32)
        for d in range(2):
            mask = masks[d]
            u = gl_ref[d, U, pl.ds(c, 1), :]
            lf = gl_ref[d, LF, pl.ds(c, 1), :]
            m_prev = gl_ref[d, MPREV, pl.ds(c, 1), :]
            um = jnp.where(mask, u, NEG_INF)
            mloc = jnp.maximum(jnp.max(um, axis=-1, keepdims=True), m_prev)
            dt = jnp.exp2(um - mloc)
            inter = jnp.exp2(m_prev - mloc)
            lhs = jnp.concatenate([(s * dt).astype(BF16), (qf * inter).astype(BF16)], axis=1)
            rhs = jnp.concatenate([va, cprev_ref[d, c]], axis=0)
            out = jnp.dot(lhs, rhs, preferred_element_type=F32)
            b_col = jnp.sum(jnp.where(mask, lf, 0.0), axis=-1, keepdims=True)
            stab = jnp.exp2(-(b_col + mloc))
            hm = hm + out[:, 0:LANES] / jnp.maximum(jnp.abs(out[:, LANES:]), stab)
        ms = jnp.mean(hm * hm, axis=-1, keepdims=True)
        y = hm * lax.rsqrt(ms + NORM_EPS) * hg_ref[...] * o_ref[pl.ds(r0, CHUNK), :]
        y_ref[pl.ds(r0, CHUNK), :] = y.astype(y_ref.dtype)
        return carry

    lax.fori_loop(0, n_chunks, out_body, 0, unroll=OUT_UNROLL)


def _mlstm(proj3, gates_row, gate_bias, conv_w, head_norm_g, layer):
    batch, seq, _ = proj3.shape
    n_chunks = seq // CHUNK
    col = lambda off: pl.BlockSpec((None, seq, LANES), lambda b, h: (b, 0, off + h))
    gate = lambda off: pl.BlockSpec((None, None, n_chunks, LANES), lambda b, h: (b, off + h, 0, 0))
    return pl.pallas_call(
        functools.partial(_mlstm_kernel, bias_base=layer * N_GATES),
        grid=(batch, M_HEADS),
        in_specs=[
            pl.BlockSpec(memory_space=pltpu.SMEM),
            col(0), col(M_HEADS), col(2 * M_HEADS), col(3 * M_HEADS),
            gate(0), gate(M_HEADS), gate(2 * M_HEADS), gate(3 * M_HEADS),
            pl.BlockSpec((None, CONV_WIDTH, LANES), lambda b, h: (layer, 0, h)),
            pl.BlockSpec((None, CONV_WIDTH, LANES), lambda b, h: (layer, 0, M_HEADS + h)),
            pl.BlockSpec((None, 1, LANES), lambda b, h: (layer, 0, h)),
        ],
        out_specs=pl.BlockSpec((None, seq, LANES), lambda b, h: (b, 0, h)),
        out_shape=jax.ShapeDtypeStruct((batch, seq, M_WIDTH), BF16),
        scratch_shapes=[
            pltpu.VMEM((seq + 16, LANES), F32),
            pltpu.VMEM((seq, LANES), BF16),
            pltpu.VMEM((seq, LANES), F32),
            pltpu.VMEM((seq, 2 * LANES), BF16),
            pltpu.VMEM((2, M_HEAD_DIM, 2 * LANES), F32),
            pltpu.VMEM((2, n_chunks, M_HEAD_DIM, 2 * LANES), BF16),
            pltpu.VMEM((2, N_GATE_ROWS, n_chunks, LANES), F32),
        ],
        compiler_params=pltpu.CompilerParams(
            dimension_semantics=("arbitrary", "arbitrary"), vmem_limit_bytes=VMEM_LIMIT),
        name="mlstm",
    )(gate_bias, proj3, proj3, proj3, proj3, gates_row, gates_row, gates_row, gates_row,
      conv_w, conv_w, head_norm_g)


ATTN_UNROLL = 16


def _attn_kernel(q_ref, k_ref, v_ref, bias_ref, bias8_ref, y_ref,
                 c4_ref, c8_ref, num_ref, den_ref, mx_ref):
    seq = q_ref.shape[0]
    rope_rows = 256

    d4, d16 = DILATIONS[1], DILATIONS[2]
    d8 = d16 // 2
    len4, len8 = seq // d4, seq // d8

    def split4_body(i, carry):
        r = i // (len4 // rope_rows)
        l0 = (i % (len4 // rope_rows)) * rope_rows
        dst = pl.ds(pl.multiple_of(r * len4 + l0, rope_rows), rope_rows)
        for t, src in enumerate((q_ref, k_ref, v_ref)):
            c4_ref[t, dst, :] = src[pl.ds(r + d4 * l0, rope_rows, stride=d4), :]
        return carry

    lax.fori_loop(0, seq // rope_rows, split4_body, 0)

    def split8_body(r, carry):
        src_rows = pl.ds((r % d4) * len4 + r // d4, len8, stride=d8 // d4)
        dst = pl.ds(pl.multiple_of(r * len8, len8), len8)
        for t in range(3):
            c8_ref[t, dst, :] = c4_ref[t, src_rows, :]
        return carry

    lax.fori_loop(0, d8, split8_body, 0)

    lane_q = lax.broadcasted_iota(jnp.int32, (QB, LANES), 1)
    head0_q = lane_q < A_HEAD_DIM
    head0_qk = (lane_q // (A_HEAD_DIM // 2)) % 2 == 0

    def block(q, k, v, bias):
        span = k.shape[0]
        q2 = jnp.concatenate([jnp.where(head0_qk, q, 0.0), jnp.where(head0_qk, 0.0, q)], axis=0).astype(BF16)
        s = lax.dot_general(q2, k.astype(BF16), (((1,), (1,)), ((), ())), preferred_element_type=F32)
        s = s + jnp.concatenate([bias, bias], axis=0)
        m = jnp.max(s, axis=-1, keepdims=True)
        p = jnp.exp2(s - m).astype(BF16)
        rhs = jnp.concatenate([v.astype(BF16), jnp.ones((span, LANES), BF16)], axis=1)
        acc = jnp.dot(p, rhs, preferred_element_type=F32)
        return (jnp.where(head0_q, acc[0:QB, 0:LANES], acc[QB:, 0:LANES]),
                jnp.where(head0_q, acc[0:QB, LANES:], acc[QB:, LANES:]),
                jnp.where(head0_q, m[0:QB], m[QB:2 * QB]))

    stats_refs = (num_ref, den_ref, mx_ref)

    def band_body(i, carry, pat, mod, half, srcs, biases):
        length = seq // mod
        n_blocks = length // QB
        r = i // n_blocks
        j = i % n_blocks
        q0 = j * QB
        k0 = jnp.clip(q0 - half, 0, length - SPAN)
        variant = (q0 - k0) // half
        qrows = pl.ds(pl.multiple_of(r * length + q0, QB), QB)
        krows = pl.ds(pl.multiple_of(r * length + k0, HALF_WINDOW), SPAN)
        if mod == 1:
            out_rows = qrows
        else:
            out_rows = pl.ds(r + mod * q0, QB, stride=mod)
        res = block(srcs[0](qrows), srcs[1](krows), srcs[2](krows), biases[variant])
        for ref, val in zip(stats_refs, res):
            ref[pat, out_rows, :] = val
        return carry

    nat = (lambda rows: q_ref[rows, :], lambda rows: k_ref[rows, :], lambda rows: v_ref[rows, :])
    cm4 = tuple((lambda rows, t=t: c4_ref[t, rows, :]) for t in range(3))
    cm8 = tuple((lambda rows, t=t: c8_ref[t, rows, :]) for t in range(3))
    for pat, (mod, half, srcs, biases) in enumerate((
            (DILATIONS[0], HALF_WINDOW, nat, bias_ref),
            (d4, HALF_WINDOW, cm4, bias_ref),
            (d8, HALF_WINDOW * (d16 // d8), cm8, bias8_ref))):
        lax.fori_loop(0, seq // QB,
                      functools.partial(band_body, pat=pat, mod=mod, half=half, srcs=srcs, biases=biases),
                      0, unroll=ATTN_UNROLL)

    def merge_body(c, carry):
        rows = pl.ds(pl.multiple_of(c * rope_rows, rope_rows), rope_rows)
        m_all = jnp.maximum(jnp.maximum(mx_ref[0, rows, :], mx_ref[1, rows, :]), mx_ref[2, rows, :])
        num = jnp.zeros((rope_rows, LANES), F32)
        den = jnp.zeros((rope_rows, LANES), F32)
        for pat in range(len(DILATIONS)):
            w = jnp.exp2(mx_ref[pat, rows, :] - m_all)
            num = num + w * num_ref[pat, rows, :]
            den = den + w * den_ref[pat, rows, :]
        y_ref[rows, :] = (num / den).astype(y_ref.dtype)
        return carry

    lax.fori_loop(0, seq // rope_rows, merge_body, 0)


def _attention(proj3, bias, bias8):
    batch, seq, _ = proj3.shape
    assert DILATIONS == (1, 4, 16) and seq * 2 // DILATIONS[2] == SPAN
    n_pairs = A_WIDTH // LANES
    base = 4 * M_WIDTH // LANES
    col = lambda off: pl.BlockSpec((None, seq, LANES), lambda b, p: (b, 0, base + off + p))
    const2 = lambda shape: pl.BlockSpec(shape, lambda b, p: (0,) * len(shape))
    n_pat = len(DILATIONS)
    return pl.pallas_call(
        _attn_kernel,
        grid=(batch, n_pairs),
        in_specs=[
            col(0), col(n_pairs), col(2 * n_pairs),
            const2((3, QB, SPAN)), const2((2, QB, SPAN)),
        ],
        out_specs=pl.BlockSpec((None, seq, LANES), lambda b, p: (b, 0, p)),
        out_shape=jax.ShapeDtypeStruct((batch, seq, A_WIDTH), BF16),
        scratch_shapes=[
            pltpu.VMEM((3, seq, LANES), F32),
            pltpu.VMEM((3, seq, LANES), F32),
            pltpu.VMEM((n_pat, seq, LANES), F32),
            pltpu.VMEM((n_pat, seq, LANES), F32),
            pltpu.VMEM((n_pat, seq, LANES), F32),
        ],
        compiler_params=pltpu.CompilerParams(
            dimension_semantics=("arbitrary", "arbitrary"), vmem_limit_bytes=VMEM_LIMIT),
        name="dilated_attn",
    )(proj3, proj3, proj3, bias, bias8)


def _out_mlp_kernel(x_ref, ym_ref, ya_ref, wo_ref, g2_ref, wu_ref, wd_ref, fg_ref, o_ref, usq_ref,
                    *, ff_chunk, final):
    x1 = (x_ref[...]
          + jnp.dot(ym_ref[...], wo_ref[0:M_WIDTH, :], preferred_element_type=F32)
          + jnp.dot(ya_ref[...], wo_ref[M_WIDTH:, :], preferred_element_type=F32))
    ms = jnp.mean(x1 * x1, axis=-1, keepdims=True)
    h = (x1 * lax.rsqrt(ms + NORM_EPS) * g2_ref[...]).astype(BF16)
    for c in range(0, D_FF, ff_chunk):
        u = jnp.maximum(jnp.dot(h, wu_ref[:, c:c + ff_chunk], preferred_element_type=F32), 0.0)
        usq_ref[:, c:c + ff_chunk] = (u * u).astype(BF16)
    out = x1 + jnp.dot(usq_ref[...], wd_ref[...], preferred_element_type=F32)
    if final:
        ms = jnp.mean(out * out, axis=-1, keepdims=True)
        out = out * lax.rsqrt(ms + NORM_EPS) * fg_ref[...]
    o_ref[...] = out


def _out_mlp(x2d, ym, ya, w_out, g2, w_up, w_down, final_g, layer, final, tm=512):
    m = x2d.shape[0]
    resident = lambda shape: pl.BlockSpec((None,) + shape, lambda i: (layer, 0, 0), pipeline_mode=pl.Buffered(1))
    return pl.pallas_call(
        functools.partial(_out_mlp_kernel, ff_chunk=1024, final=final),
        grid=(m // tm,),
        in_specs=[
            pl.BlockSpec((tm, D_MODEL), lambda i: (i, 0)),
            pl.BlockSpec((tm, M_WIDTH), lambda i: (i, 0)),
            pl.BlockSpec((tm, A_WIDTH), lambda i: (i, 0)),
            resident((D_MODEL, D_MODEL)),
            resident((1, D_MODEL)),
            resident((D_MODEL, D_FF)),
            resident((D_FF, D_MODEL)),
            pl.BlockSpec((1, D_MODEL), lambda i: (0, 0), pipeline_mode=pl.Buffered(1)),
        ],
        out_specs=pl.BlockSpec((tm, D_MODEL), lambda i: (i, 0)),
        out_shape=jax.ShapeDtypeStruct((m, D_MODEL), F32),
        scratch_shapes=[pltpu.VMEM((tm, D_FF), BF16)],
        compiler_params=pltpu.CompilerParams(
            dimension_semantics=("arbitrary",), vmem_limit_bytes=VMEM_LIMIT),
        name="out_mlp",
    )(x2d, ym, ya, w_out, g2, w_up, w_down, final_g)


def _rope_tables(seq):
    half = A_HEAD_DIM // 2
    inv_freq = ROPE_THETA ** (-jnp.arange(half, dtype=F32) / half)
    ang = jnp.arange(seq, dtype=F32)[:, None] * inv_freq[None, :]
    cos = jnp.tile(jnp.cos(ang), (1, LANES // half))
    sin = jnp.sin(ang)
    sin = jnp.concatenate([-sin, -sin, sin, sin], axis=1)
    q_scale = LOG2E / math.sqrt(A_HEAD_DIM)
    return jnp.concatenate([cos * q_scale, sin * q_scale, cos, sin], axis=1)


def _band_bias():
    row = jnp.arange(QB)[:, None]
    col = jnp.arange(SPAN)[None, :]
    variants = []
    for off in (0, HALF_WINDOW, 2 * HALF_WINDOW):
        rel = col - off - row
        variants.append(jnp.where(jnp.abs(rel) <= HALF_WINDOW, 0.0, NEG_INF))
    variants8 = []
    for off in (0, QB):
        rel = col - off - row
        variants8.append(jnp.where((jnp.abs(rel) <= 2 * HALF_WINDOW) & (rel % 2 == 0), 0.0, NEG_INF))
    return jnp.stack(variants).astype(F32), jnp.stack(variants8).astype(F32)


def kernel(x, norm1_g, w_in, conv_w, gate_i_b, gate_f_b, head_norm_g, w_out, norm2_g, w_up, w_down, final_g):
    batch, seq, _ = x.shape
    depth = w_in.shape[0]
    n_chunks = seq // CHUNK
    rope_tab = _rope_tables(seq)
    bias, bias8 = _band_bias()
    gate_lo = 4 * M_WIDTH
    attn_lo = gate_lo + N_GATES

    half = A_HEAD_DIM // 2
    w_qk = w_in[:, :, attn_lo:attn_lo + 2 * A_WIDTH].reshape(depth, D_MODEL, -1, 2, 2, half)
    w_qk = jnp.swapaxes(w_qk, 3, 4).reshape(depth, D_MODEL, 2 * A_WIDTH)
    w_cat = jnp.concatenate(
        [w_in[:, :, :gate_lo], w_qk, w_in[:, :, attn_lo + 2 * A_WIDTH:],
         w_in[:, :, gate_lo:attn_lo], jnp.zeros((depth, D_MODEL, LANES - N_GATES), F32)], axis=2).astype(BF16)
    w_out_b, w_up_b, w_down_b = w_out.astype(BF16), w_up.astype(BF16), w_down.astype(BF16)
    gate_bias = jnp.concatenate([gate_i_b, gate_f_b], axis=1).reshape(depth * N_GATES).astype(F32)
    norm1_3, norm2_3, head_g3 = norm1_g[:, None, :], norm2_g[:, None, :], head_norm_g[:, None, :]

    x2d = x.reshape(batch * seq, D_MODEL)
    for layer in range(depth):
        proj = _in_proj(x2d, norm1_3, w_cat, rope_tab, layer, seq)
        proj3 = proj.reshape(batch, seq, PROJ_COLS)
        gates_row = jnp.transpose(proj3[:, :, MAIN_COLS:MAIN_COLS + N_GATES], (0, 2, 1)).reshape(
            batch, N_GATES, n_chunks, CHUNK)
        ym = _mlstm(proj3, gates_row, gate_bias, conv_w, head_g3, layer)
        ya = _attention(proj3, bias, bias8)
        x2d = _out_mlp(x2d, ym.reshape(batch * seq, M_WIDTH), ya.reshape(batch * seq, A_WIDTH),
                       w_out_b, norm2_3, w_up_b, w_down_b, final_g[None, :],
                       layer, final=(layer == depth - 1))
    return x2d.reshape(batch, seq, D_MODEL)
```

```python
import functools
import math

import jax
import jax.numpy as jnp
from jax import lax
from jax.experimental import pallas as pl
from jax.experimental.pallas import tpu as pltpu

F32 = jnp.float32
BF16 = jnp.bfloat16

D_MODEL = 1024
M_HEADS = 4
M_HEAD_DIM = 128
M_WIDTH = M_HEADS * M_HEAD_DIM
A_HEADS = 8
A_HEAD_DIM = 64
A_WIDTH = A_HEADS * A_HEAD_DIM
D_FF = 4 * D_MODEL
CONV_WIDTH = 5
ROPE_THETA = 10000.0
NORM_EPS = 1e-6
NEG_INF = -1e30
DILATIONS = (1, 4, 16)
HALF_WINDOW = 64

LANES = 128
N_GATES = 4 * M_HEADS
MAIN_COLS = 4 * M_WIDTH + 3 * A_WIDTH
PROJ_COLS = MAIN_COLS + LANES
MV_COL = 2 * M_WIDTH
OGATE_COL = 3 * M_WIDTH
AQ_COL = 4 * M_WIDTH
AK_COL = AQ_COL + A_WIDTH
CHUNK = 128
QB = 128
SPAN = QB + 2 * HALF_WINDOW
VMEM_LIMIT = 56 * 1024 * 1024
LOG2E = 1.4426950408889634
N_GATE_ROWS = 5
STATE_UNROLL = 4
OUT_UNROLL = 8


def _sigmoid(x):
    return 1.0 / (1.0 + jnp.exp(-x))


def _in_proj_kernel(x_ref, g_ref, w_ref, rope_ref, o_ref, va_ref, gt_ref):
    x = x_ref[...]
    ms = jnp.mean(x * x, axis=-1, keepdims=True)
    h = (x * lax.rsqrt(ms + NORM_EPS) * g_ref[...]).astype(BF16)
    for c in range(0, PROJ_COLS, M_WIDTH):
        w = min(M_WIDTH, PROJ_COLS - c)
        r = jnp.dot(h, w_ref[:, c:c + w], preferred_element_type=F32)
        if c == OGATE_COL:
            r = _sigmoid(r)
        if c == MV_COL:
            for hd in range(M_HEADS):
                va_ref[:, 2 * hd * LANES:(2 * hd + 1) * LANES] = r[:, hd * LANES:(hd + 1) * LANES].astype(BF16)
                va_ref[:, (2 * hd + 1) * LANES:(2 * hd + 2) * LANES] = jnp.ones((r.shape[0], LANES), BF16)
        if c == MAIN_COLS:
            gt_ref[...] = r.T
        if c in (AQ_COL, AK_COL):
            t = 0 if c == AQ_COL else 2
            cos = rope_ref[:, t * LANES:(t + 1) * LANES]
            sin = rope_ref[:, (t + 1) * LANES:(t + 2) * LANES]
            for j in range(0, w, LANES):
                xj = r[:, j:j + LANES]
                o_ref[:, c + j:c + j + LANES] = xj * cos + pltpu.roll(xj, LANES // 2, axis=1) * sin
        else:
            o_ref[:, c:c + w] = r


def _in_proj(x2d, g, w, rope_tab, layer, seq, tm=512):
    m = x2d.shape[0]
    tiles_per_seq = seq // tm
    return pl.pallas_call(
        _in_proj_kernel,
        grid=(m // tm,),
        in_specs=[
            pl.BlockSpec((tm, D_MODEL), lambda i: (i, 0)),
            pl.BlockSpec((None, 1, D_MODEL), lambda i: (layer, 0, 0)),
            pl.BlockSpec((None, D_MODEL, PROJ_COLS), lambda i: (layer, 0, 0), pipeline_mode=pl.Buffered(1)),
            pl.BlockSpec((tm, 4 * LANES), lambda i: (i % tiles_per_seq, 0)),
        ],
        out_specs=[
            pl.BlockSpec((tm, PROJ_COLS), lambda i: (i, 0)),
            pl.BlockSpec((tm, 2 * M_WIDTH), lambda i: (i, 0)),
            pl.BlockSpec((LANES, tm), lambda i: (0, i)),
        ],
        out_shape=[
            jax.ShapeDtypeStruct((m, PROJ_COLS), F32),
            jax.ShapeDtypeStruct((m, 2 * M_WIDTH), BF16),
            jax.ShapeDtypeStruct((LANES, m), F32),
        ],
        compiler_params=pltpu.CompilerParams(
            dimension_semantics=("arbitrary",), vmem_limit_bytes=VMEM_LIMIT),
        name="in_proj",
    )(x2d, g, w, rope_tab)


def _lane_cumsum(x, reverse):
    lane = lax.broadcasted_iota(jnp.int32, x.shape, 1)
    k = 1
    while k < LANES:
        if reverse:
            x = x + jnp.where(lane < LANES - k, pltpu.roll(x, LANES - k, axis=1), 0.0)
        else:
            x = x + jnp.where(lane >= k, pltpu.roll(x, k, axis=1), 0.0)
        k *= 2
    return x


def _mlstm_kernel(bias_ref, q_ref, k_ref, va_ref, o_ref, li_f_ref, li_b_ref, lf_f_ref, lf_b_ref,
                  cwq_ref, cwk_ref, hg_ref, y_ref,
                  edge_ref, qc_ref, kc_ref, cst_ref, cprev_ref, gl_ref, *, bias_base):
    seq = q_ref.shape[0]
    n_chunks = seq // CHUNK
    head = pl.program_id(1)
    pad = CONV_WIDTH // 2

    def conv_into(src_ref, cw_ref, dst_ref, scale, lo, hi):
        def conv_chunk(window, r0):
            acc = jnp.zeros((CHUNK, LANES), F32)
            for t in range(CONV_WIDTH):
                acc = acc + cw_ref[t:t + 1, :] * window(t - pad)
            act = acc * _sigmoid(acc)
            if scale != 1.0:
                act = act * scale
            dst_ref[pl.ds(r0, CHUNK), :] = act.astype(dst_ref.dtype)

        edge_ref[lo, 0:8, :] = jnp.zeros((8, LANES), F32)
        edge_ref[lo, 8:8 + CHUNK + 8, :] = src_ref[0:CHUNK + 8, :]
        edge_ref[hi, 0:CHUNK + 8, :] = src_ref[seq - CHUNK - 8:seq, :]
        edge_ref[hi, CHUNK + 8:CHUNK + 16, :] = jnp.zeros((8, LANES), F32)
        conv_chunk(lambda off: edge_ref[lo, 8 + off:8 + off + CHUNK, :], 0)
        conv_chunk(lambda off: edge_ref[hi, 8 + off:8 + off + CHUNK, :], seq - CHUNK)

        for c in range(1, n_chunks - 1):
            r0 = c * CHUNK
            conv_chunk(lambda off, r0=r0: src_ref[r0 + off:r0 + off + CHUNK, :], r0)

    U, NLF, MPREV, WROW, DECAY = range(N_GATE_ROWS)
    chunk_id = lax.broadcasted_iota(jnp.int32, (n_chunks, LANES), 0)
    for d, (li_ref, lf_ref) in enumerate(((li_f_ref, lf_f_ref), (li_b_ref, lf_b_ref))):
        li = (li_ref[...] + bias_ref[bias_base + d * M_HEADS + head]) * LOG2E
        fpre = lf_ref[...] + bias_ref[bias_base + 2 * M_HEADS + d * M_HEADS + head]
        lf = (jnp.minimum(fpre, 0.0) - jnp.log(1.0 + jnp.exp(-jnp.abs(fpre)))) * LOG2E
        b = _lane_cumsum(lf, reverse=(d == 1))
        u = li - b
        tot = jnp.broadcast_to(jnp.sum(lf, axis=-1, keepdims=True), lf.shape)
        gmax = jnp.broadcast_to(jnp.max(u, axis=-1, keepdims=True), lf.shape) + tot
        m = jnp.full((1, LANES), NEG_INF, F32)
        m_prev = jnp.zeros((n_chunks, LANES), F32)
        m_new = jnp.zeros((n_chunks, LANES), F32)
        for step in range(n_chunks):
            c = step if d == 0 else n_chunks - 1 - step
            m_prev = jnp.where(chunk_id == c, m, m_prev)
            m = jnp.maximum(tot[c:c + 1, :] + m, gmax[c:c + 1, :])
            m_new = jnp.where(chunk_id == c, m, m_new)
        gl_ref[d, U] = u
        gl_ref[d, NLF] = -lf
        gl_ref[d, MPREV] = m_prev
        gl_ref[d, WROW] = jnp.exp2(u + tot - m_new)
        gl_ref[d, DECAY] = jnp.exp2(tot + m_prev - m_new)

    conv_into(q_ref, cwq_ref, qc_ref, 1.0, 0, 1)
    conv_into(k_ref, cwk_ref, kc_ref, 1.0 / math.sqrt(M_HEAD_DIM), 2, 3)

    cst_ref[...] = jnp.zeros(cst_ref.shape, F32)

    def state_body(i, carry):
        for d in range(2):
            c = i if d == 0 else n_chunks - 1 - i
            r0 = pl.multiple_of(c * CHUNK, CHUNK)
            k = kc_ref[pl.ds(r0, CHUNK), :]
            va = va_ref[pl.ds(r0, CHUNK), :]
            w = gl_ref[d, WROW, pl.ds(c, 1), :]
            decay = gl_ref[d, DECAY, pl.ds(c, 1), :]
            upd = jnp.dot((k.T * w).astype(BF16), va, preferred_element_type=F32)
            state = cst_ref[d]
            cprev_ref[d, c] = state.astype(BF16)
            cst_ref[d] = jnp.concatenate([decay, decay], axis=1) * state + upd
        return carry

    lax.fori_loop(0, n_chunks, state_body, 0, unroll=STATE_UNROLL)

    row = lax.broadcasted_iota(jnp.int32, (CHUNK, CHUNK), 0)
    col = lax.broadcasted_iota(jnp.int32, (CHUNK, CHUNK), 1)
    masks = (col <= row, col >= row)

    def out_body(c, carry):
        r0 = pl.multiple_of(c * CHUNK, CHUNK)
        q = qc_ref[pl.ds(r0, CHUNK), :]
        k = kc_ref[pl.ds(r0, CHUNK), :]
        va = va_ref[pl.ds(r0, CHUNK), :]
        s = lax.dot_general(q, k.astype(BF16), (((1,), (1,)), ((), ())), preferred_element_type=F32)
        hm = jnp.zeros((CHUNK, LANES), F32)
        for d in range(2):
            mask = masks[d]
            u = gl_ref[d, U, pl.ds(c, 1), :]
            nlf = gl_ref[d, NLF, pl.ds(c, 1), :]
            m_prev = gl_ref[d, MPREV, pl.ds(c, 1), :]
            um = jnp.where(mask, u, NEG_INF)
            mloc = jnp.maximum(jnp.max(um, axis=-1, keepdims=True), m_prev)
            dt = jnp.exp2(um - mloc)
            inter = jnp.exp2(m_prev - mloc)
            lhs = jnp.concatenate([(s * dt).astype(BF16), q * inter.astype(BF16)], axis=1)
            rhs = jnp.concatenate([va, cprev_ref[d, c]], axis=0)
            out = jnp.dot(lhs, rhs, preferred_element_type=F32)
            nb_col = jnp.sum(jnp.where(mask, nlf, 0.0), axis=-1, keepdims=True)
            stab = jnp.exp2(nb_col - mloc)
            hm = hm + out[:, 0:LANES] / jnp.maximum(jnp.abs(out[:, LANES:]), stab)
        ms = jnp.mean(hm * hm, axis=-1, keepdims=True)
        y = hm * lax.rsqrt(ms + NORM_EPS) * hg_ref[...] * o_ref[pl.ds(r0, CHUNK), :]
        y_ref[pl.ds(r0, CHUNK), :] = y.astype(y_ref.dtype)
        return carry

    lax.fori_loop(0, n_chunks, out_body, 0, unroll=OUT_UNROLL)


def _mlstm(proj3, va3, gates_t, gate_bias, conv_w, head_norm_g, layer):
    batch, seq, _ = proj3.shape
    n_chunks = seq // CHUNK
    col = lambda off: pl.BlockSpec((None, seq, LANES), lambda b, h: (b, 0, off + h))
    gate = lambda off: pl.BlockSpec((None, None, n_chunks, LANES), lambda b, h: (off + h, b, 0, 0))
    return pl.pallas_call(
        functools.partial(_mlstm_kernel, bias_base=layer * N_GATES),
        grid=(batch, M_HEADS),
        in_specs=[
            pl.BlockSpec(memory_space=pltpu.SMEM),
            col(0), col(M_HEADS),
            pl.BlockSpec((None, seq, 2 * LANES), lambda b, h: (b, 0, h)),
            col(3 * M_HEADS),
            gate(0), gate(M_HEADS), gate(2 * M_HEADS), gate(3 * M_HEADS),
            pl.BlockSpec((None, CONV_WIDTH, LANES), lambda b, h: (layer, 0, h)),
            pl.BlockSpec((None, CONV_WIDTH, LANES), lambda b, h: (layer, 0, M_HEADS + h)),
            pl.BlockSpec((None, 1, LANES), lambda b, h: (layer, 0, h)),
        ],
        out_specs=pl.BlockSpec((None, seq, LANES), lambda b, h: (b, 0, h)),
        out_shape=jax.ShapeDtypeStruct((batch, seq, M_WIDTH), BF16),
        scratch_shapes=[
            pltpu.VMEM((4, CHUNK + 16, LANES), F32),
            pltpu.VMEM((seq, LANES), BF16),
            pltpu.VMEM((seq, LANES), F32),
            pltpu.VMEM((2, M_HEAD_DIM, 2 * LANES), F32),
            pltpu.VMEM((2, n_chunks, M_HEAD_DIM, 2 * LANES), BF16),
            pltpu.VMEM((2, N_GATE_ROWS, n_chunks, LANES), F32),
        ],
        compiler_params=pltpu.CompilerParams(
            dimension_semantics=("arbitrary", "arbitrary"), vmem_limit_bytes=VMEM_LIMIT),
        name="mlstm",
    )(gate_bias, proj3, proj3, va3, proj3, gates_t, gates_t, gates_t, gates_t,
      conv_w, conv_w, head_norm_g)


ATTN_UNROLL = 16


def _attn_kernel(q_ref, k_ref, v_ref, bias_ref, bias8_ref, y_ref,
                 c4_ref, c8_ref, num_ref, den_ref, mx_ref):
    seq = q_ref.shape[0]
    rope_rows = 256

    d4, d16 = DILATIONS[1], DILATIONS[2]
    d8 = d16 // 2
    len4, len8 = seq // d4, seq // d8

    def split4_body(i, carry):
        r = i // (len4 // rope_rows)
        l0 = (i % (len4 // rope_rows)) * rope_rows
        dst = pl.ds(pl.multiple_of(r * len4 + l0, rope_rows), rope_rows)
        for t, src in enumerate((q_ref, k_ref, v_ref)):
            c4_ref[t, dst, :] = src[pl.ds(r + d4 * l0, rope_rows, stride=d4), :]
        return carry

    lax.fori_loop(0, seq // rope_rows, split4_body, 0)

    def split8_body(r, carry):
        src_rows = pl.ds((r % d4) * len4 + r // d4, len8, stride=d8 // d4)
        dst = pl.ds(pl.multiple_of(r * len8, len8), len8)
        for t in range(3):
            c8_ref[t, dst, :] = c4_ref[t, src_rows, :]
        return carry

    lax.fori_loop(0, d8, split8_body, 0)

    lane_q = lax.broadcasted_iota(jnp.int32, (QB, LANES), 1)
    head0_q = lane_q < A_HEAD_DIM
    head0_qk = (lane_q // (A_HEAD_DIM // 2)) % 2 == 0

    def block(q, k, v, bias):
        span = k.shape[0]
        q2 = jnp.concatenate([jnp.where(head0_qk, q, 0.0), jnp.where(head0_qk, 0.0, q)], axis=0).astype(BF16)
        s = lax.dot_general(q2, k.astype(BF16), (((1,), (1,)), ((), ())), preferred_element_type=F32)
        s = s + jnp.concatenate([bias, bias], axis=0)
        m = jnp.max(s, axis=-1, keepdims=True)
        p = jnp.exp2(s - m).astype(BF16)
        rhs = jnp.concatenate([v.astype(BF16), jnp.ones((span, LANES), BF16)], axis=1)
        acc = jnp.dot(p, rhs, preferred_element_type=F32)
        return (jnp.where(head0_q, acc[0:QB, 0:LANES], acc[QB:, 0:LANES]),
                jnp.where(head0_q, acc[0:QB, LANES:], acc[QB:, LANES:]),
                jnp.where(head0_q, m[0:QB], m[QB:2 * QB]))

    stats_refs = (num_ref, den_ref, mx_ref)

    def band_body(i, carry, pat, mod, half, srcs, biases):
        length = seq // mod
        n_blocks = length // QB
        r = i // n_blocks
        j = i % n_blocks
        q0 = j * QB
        k0 = jnp.clip(q0 - half, 0, length - SPAN)
        variant = (q0 - k0) // half
        qrows = pl.ds(pl.multiple_of(r * length + q0, QB), QB)
        krows = pl.ds(pl.multiple_of(r * length + k0, HALF_WINDOW), SPAN)
        if mod == 1:
            out_rows = qrows
        else:
            out_rows = pl.ds(r + mod * q0, QB, stride=mod)
        res = block(srcs[0](qrows), srcs[1](krows), srcs[2](krows), biases[variant])
        for ref, val in zip(stats_refs, res):
            ref[pat, out_rows, :] = val
        return carry

    nat = (lambda rows: q_ref[rows, :], lambda rows: k_ref[rows, :], lambda rows: v_ref[rows, :])
    cm4 = tuple((lambda rows, t=t: c4_ref[t, rows, :]) for t in range(3))
    cm8 = tuple((lambda rows, t=t: c8_ref[t, rows, :]) for t in range(3))
    for pat, (mod, half, srcs, biases) in enumerate((
            (DILATIONS[0], HALF_WINDOW, nat, bias_ref),
            (d4, HALF_WINDOW, cm4, bias_ref),
            (d8, HALF_WINDOW * (d16 // d8), cm8, bias8_ref))):
        lax.fori_loop(0, seq // QB,
                      functools.partial(band_body, pat=pat, mod=mod, half=half, srcs=srcs, biases=biases),
                      0, unroll=ATTN_UNROLL)

    def merge_body(c, carry):
        rows = pl.ds(pl.multiple_of(c * rope_rows, rope_rows), rope_rows)
        m_all = jnp.maximum(jnp.maximum(mx_ref[0, rows, :], mx_ref[1, rows, :]), mx_ref[2, rows, :])
        num = jnp.zeros((rope_rows, LANES), F32)
        den = jnp.zeros((rope_rows, LANES), F32)
        for pat in range(len(DILATIONS)):
            w = jnp.exp2(mx_ref[pat, rows, :] - m_all)
            num = num + w * num_ref[pat, rows, :]
            den = den + w * den_ref[pat, rows, :]
        y_ref[rows, :] = (num / den).astype(y_ref.dtype)
        return carry

    lax.fori_loop(0, seq // rope_rows, merge_body, 0)


def _attention(proj3, bias, bias8):
    batch, seq, _ = proj3.shape
    assert DILATIONS == (1, 4, 16) and seq * 2 // DILATIONS[2] == SPAN
    n_pairs = A_WIDTH // LANES
    base = 4 * M_WIDTH // LANES
    col = lambda off: pl.BlockSpec((None, seq, LANES), lambda b, p: (b, 0, base + off + p))
    const2 = lambda shape: pl.BlockSpec(shape, lambda b, p: (0,) * len(shape))
    n_pat = len(DILATIONS)
    return pl.pallas_call(
        _attn_kernel,
        grid=(batch, n_pairs),
        in_specs=[
            col(0), col(n_pairs), col(2 * n_pairs),
            const2((3, QB, SPAN)), const2((2, QB, SPAN)),
        ],
        out_specs=pl.BlockSpec((None, seq, LANES), lambda b, p: (b, 0, p)),
        out_shape=jax.ShapeDtypeStruct((batch, seq, A_WIDTH), BF16),
        scratch_shapes=[
            pltpu.VMEM((3, seq, LANES), F32),
            pltpu.VMEM((3, seq, LANES), F32),
            pltpu.VMEM((n_pat, seq, LANES), F32),
            pltpu.VMEM((n_pat, seq, LANES), F32),
            pltpu.VMEM((n_pat, seq, LANES), F32),
        ],
        compiler_params=pltpu.CompilerParams(
            dimension_semantics=("arbitrary", "arbitrary"), vmem_limit_bytes=VMEM_LIMIT),
        name="dilated_attn",
    )(proj3, proj3, proj3, bias, bias8)


def _out_mlp_kernel(x_ref, ym_ref, ya_ref, wo_ref, g2_ref, wu_ref, wd_ref, fg_ref, o_ref, usq_ref,
                    *, ff_chunk, final):
    x1 = (x_ref[...]
          + jnp.dot(ym_ref[...], wo_ref[0:M_WIDTH, :], preferred_element_type=F32)
          + jnp.dot(ya_ref[...], wo_ref[M_WIDTH:, :], preferred_element_type=F32))
    ms = jnp.mean(x1 * x1, axis=-1, keepdims=True)
    h = (x1 * lax.rsqrt(ms + NORM_EPS) * g2_ref[...]).astype(BF16)
    for c in range(0, D_FF, ff_chunk):
        u = jnp.maximum(jnp.dot(h, wu_ref[:, c:c + ff_chunk], preferred_element_type=F32), 0.0)
        usq_ref[:, c:c + ff_chunk] = (u * u).astype(BF16)
    out = x1 + jnp.dot(usq_ref[...], wd_ref[...], preferred_element_type=F32)
    if final:
        ms = jnp.mean(out * out, axis=-1, keepdims=True)
        out = out * lax.rsqrt(ms + NORM_EPS) * fg_ref[...]
    o_ref[...] = out


def _out_mlp(x2d, ym, ya, w_out, g2, w_up, w_down, final_g, layer, final, tm=512):
    m = x2d.shape[0]
    resident = lambda shape: pl.BlockSpec((None,) + shape, lambda i: (layer, 0, 0), pipeline_mode=pl.Buffered(1))
    return pl.pallas_call(
        functools.partial(_out_mlp_kernel, ff_chunk=1024, final=final),
        grid=(m // tm,),
        in_specs=[
            pl.BlockSpec((tm, D_MODEL), lambda i: (i, 0)),
            pl.BlockSpec((tm, M_WIDTH), lambda i: (i, 0)),
            pl.BlockSpec((tm, A_WIDTH), lambda i: (i, 0)),
            resident((D_MODEL, D_MODEL)),
            resident((1, D_MODEL)),
            resident((D_MODEL, D_FF)),
            resident((D_FF, D_MODEL)),
            pl.BlockSpec((1, D_MODEL), lambda i: (0, 0), pipeline_mode=pl.Buffered(1)),
        ],
        out_specs=pl.BlockSpec((tm, D_MODEL), lambda i: (i, 0)),
        out_shape=jax.ShapeDtypeStruct((m, D_MODEL), F32),
        scratch_shapes=[pltpu.VMEM((tm, D_FF), BF16)],
        compiler_params=pltpu.CompilerParams(
            dimension_semantics=("arbitrary",), vmem_limit_bytes=VMEM_LIMIT),
        name="out_mlp",
    )(x2d, ym, ya, w_out, g2, w_up, w_down, final_g)


def _rope_tables(seq):
    half = A_HEAD_DIM // 2
    inv_freq = ROPE_THETA ** (-jnp.arange(half, dtype=F32) / half)
    ang = jnp.arange(seq, dtype=F32)[:, None] * inv_freq[None, :]
    cos = jnp.tile(jnp.cos(ang), (1, LANES // half))
    sin = jnp.sin(ang)
    sin = jnp.concatenate([-sin, -sin, sin, sin], axis=1)
    q_scale = LOG2E / math.sqrt(A_HEAD_DIM)
    return jnp.concatenate([cos * q_scale, sin * q_scale, cos, sin], axis=1)


def _band_bias():
    row = jnp.arange(QB)[:, None]
    col = jnp.arange(SPAN)[None, :]
    variants = []
    for off in (0, HALF_WINDOW, 2 * HALF_WINDOW):
        rel = col - off - row
        variants.append(jnp.where(jnp.abs(rel) <= HALF_WINDOW, 0.0, NEG_INF))
    variants8 = []
    for off in (0, QB):
        rel = col - off - row
        variants8.append(jnp.where((jnp.abs(rel) <= 2 * HALF_WINDOW) & (rel % 2 == 0), 0.0, NEG_INF))
    return jnp.stack(variants).astype(F32), jnp.stack(variants8).astype(F32)


def kernel(x, norm1_g, w_in, conv_w, gate_i_b, gate_f_b, head_norm_g, w_out, norm2_g, w_up, w_down, final_g):
    batch, seq, _ = x.shape
    depth = w_in.shape[0]
    n_chunks = seq // CHUNK
    rope_tab = _rope_tables(seq)
    bias, bias8 = _band_bias()
    gate_lo = 4 * M_WIDTH
    attn_lo = gate_lo + N_GATES

    half = A_HEAD_DIM // 2
    w_qk = w_in[:, :, attn_lo:attn_lo + 2 * A_WIDTH].reshape(depth, D_MODEL, -1, 2, 2, half)
    w_qk = jnp.swapaxes(w_qk, 3, 4).reshape(depth, D_MODEL, 2 * A_WIDTH)
    w_cat = jnp.concatenate(
        [w_in[:, :, :gate_lo], w_qk, w_in[:, :, attn_lo + 2 * A_WIDTH:],
         w_in[:, :, gate_lo:attn_lo], jnp.zeros((depth, D_MODEL, LANES - N_GATES), F32)], axis=2).astype(BF16)
    w_out_b, w_up_b, w_down_b = w_out.astype(BF16), w_up.astype(BF16), w_down.astype(BF16)
    gate_bias = jnp.concatenate([gate_i_b, gate_f_b], axis=1).reshape(depth * N_GATES).astype(F32)
    norm1_3, norm2_3, head_g3 = norm1_g[:, None, :], norm2_g[:, None, :], head_norm_g[:, None, :]

    x2d = x.reshape(batch * seq, D_MODEL)
    for layer in range(depth):
        proj, va, gates_t = _in_proj(x2d, norm1_3, w_cat, rope_tab, layer, seq)
        proj3 = proj.reshape(batch, seq, PROJ_COLS)
        ym = _mlstm(proj3, va.reshape(batch, seq, 2 * M_WIDTH), gates_t.reshape(LANES, batch, n_chunks, CHUNK),
                    gate_bias, conv_w, head_g3, layer)
        ya = _attention(proj3, bias, bias8)
        x2d = _out_mlp(x2d, ym.reshape(batch * seq, M_WIDTH), ya.reshape(batch * seq, A_WIDTH),
                       w_out_b, norm2_3, w_up_b, w_down_b, final_g[None, :],
                       layer, final=(layer == depth - 1))
    return x2d.reshape(batch, seq, D_MODEL)
```

```python
import functools
import math

import jax
import jax.numpy as jnp
from jax import lax
from jax.experimental import pallas as pl
from jax.experimental.pallas import tpu as pltpu

F32 = jnp.float32
BF16 = jnp.bfloat16

D_MODEL = 1024
M_HEADS = 4
M_HEAD_DIM = 128
M_WIDTH = M_HEADS * M_HEAD_DIM
A_HEADS = 8
A_HEAD_DIM = 64
A_WIDTH = A_HEADS * A_HEAD_DIM
D_FF = 4 * D_MODEL
CONV_WIDTH = 5
ROPE_THETA = 10000.0
NORM_EPS = 1e-6
NEG_INF = -1e30
DILATIONS = (1, 4, 16)
HALF_WINDOW = 64

LANES = 128
N_GATES = 4 * M_HEADS
MAIN_COLS = 4 * M_WIDTH + 3 * A_WIDTH
PROJ_COLS = MAIN_COLS + LANES
MV_COL = 2 * M_WIDTH
OGATE_COL = 3 * M_WIDTH
AQ_COL = 4 * M_WIDTH
AK_COL = AQ_COL + A_WIDTH
CHUNK = 128
QB = 128
SPAN = QB + 2 * HALF_WINDOW
VMEM_LIMIT = 56 * 1024 * 1024
LOG2E = 1.4426950408889634
N_GATE_ROWS = 5


def _sigmoid(x):
    return 1.0 / (1.0 + jnp.exp(-x))


def _in_proj_kernel(x_ref, g_ref, w_ref, rope_ref, o_ref, va_ref, gt_ref):
    x = x_ref[...]
    ms = jnp.mean(x * x, axis=-1, keepdims=True)
    h = (x * lax.rsqrt(ms + NORM_EPS) * g_ref[...]).astype(BF16)
    for c in range(0, PROJ_COLS, M_WIDTH):
        w = min(M_WIDTH, PROJ_COLS - c)
        r = jnp.dot(h, w_ref[:, c:c + w], preferred_element_type=F32)
        if c == OGATE_COL:
            r = _sigmoid(r)
        if c == MV_COL:
            for hd in range(M_HEADS):
                va_ref[:, 2 * hd * LANES:(2 * hd + 1) * LANES] = r[:, hd * LANES:(hd + 1) * LANES].astype(BF16)
                va_ref[:, (2 * hd + 1) * LANES:(2 * hd + 2) * LANES] = jnp.ones((r.shape[0], LANES), BF16)
        if c == MAIN_COLS:
            gt_ref[...] = r.T
        if c in (AQ_COL, AK_COL):
            t = 0 if c == AQ_COL else 2
            cos = rope_ref[:, t * LANES:(t + 1) * LANES]
            sin = rope_ref[:, (t + 1) * LANES:(t + 2) * LANES]
            for j in range(0, w, LANES):
                xj = r[:, j:j + LANES]
                o_ref[:, c + j:c + j + LANES] = xj * cos + pltpu.roll(xj, LANES // 2, axis=1) * sin
        else:
            o_ref[:, c:c + w] = r


def _in_proj(x2d, g, w, rope_tab, layer, seq, tm=512):
    m = x2d.shape[0]
    tiles_per_seq = seq // tm
    return pl.pallas_call(
        _in_proj_kernel,
        grid=(m // tm,),
        in_specs=[
            pl.BlockSpec((tm, D_MODEL), lambda i: (i, 0)),
            pl.BlockSpec((None, 1, D_MODEL), lambda i: (layer, 0, 0)),
            pl.BlockSpec((None, D_MODEL, PROJ_COLS), lambda i: (layer, 0, 0), pipeline_mode=pl.Buffered(1)),
            pl.BlockSpec((tm, 4 * LANES), lambda i: (i % tiles_per_seq, 0)),
        ],
        out_specs=[
            pl.BlockSpec((tm, PROJ_COLS), lambda i: (i, 0)),
            pl.BlockSpec((tm, 2 * M_WIDTH), lambda i: (i, 0)),
            pl.BlockSpec((LANES, tm), lambda i: (0, i)),
        ],
        out_shape=[
            jax.ShapeDtypeStruct((m, PROJ_COLS), F32),
            jax.ShapeDtypeStruct((m, 2 * M_WIDTH), BF16),
            jax.ShapeDtypeStruct((LANES, m), F32),
        ],
        compiler_params=pltpu.CompilerParams(
            dimension_semantics=("arbitrary",), vmem_limit_bytes=VMEM_LIMIT),
        name="in_proj",
    )(x2d, g, w, rope_tab)


def _lane_cumsum(x, reverse):
    lane = lax.broadcasted_iota(jnp.int32, x.shape, 1)
    k = 1
    while k < LANES:
        if reverse:
            x = x + jnp.where(lane < LANES - k, pltpu.roll(x, LANES - k, axis=1), 0.0)
        else:
            x = x + jnp.where(lane >= k, pltpu.roll(x, k, axis=1), 0.0)
        k *= 2
    return x


def _mlstm_kernel(bias_ref, q_ref, k_ref, va_ref, o_ref, li_f_ref, li_b_ref, lf_f_ref, lf_b_ref,
                  cwq_ref, cwk_ref, hg_ref, y_ref,
                  edge_ref, qc_ref, kc_ref, cst_ref, cprev_ref, gl_ref, *, bias_base):
    seq = q_ref.shape[0]
    n_chunks = seq // CHUNK
    head = pl.program_id(1)
    pad = CONV_WIDTH // 2

    def conv_into(src_ref, cw_ref, dst_ref, scale, lo, hi):
        def conv_chunk(window, r0):
            acc = jnp.zeros((CHUNK, LANES), F32)
            for t in range(CONV_WIDTH):
                acc = acc + cw_ref[t:t + 1, :] * window(t - pad)
            act = acc * _sigmoid(acc)
            if scale != 1.0:
                act = act * scale
            dst_ref[pl.ds(r0, CHUNK), :] = act.astype(dst_ref.dtype)

        edge_ref[lo, 0:8, :] = jnp.zeros((8, LANES), F32)
        edge_ref[lo, 8:8 + CHUNK + 8, :] = src_ref[0:CHUNK + 8, :]
        edge_ref[hi, 0:CHUNK + 8, :] = src_ref[seq - CHUNK - 8:seq, :]
        edge_ref[hi, CHUNK + 8:CHUNK + 16, :] = jnp.zeros((8, LANES), F32)
        conv_chunk(lambda off: edge_ref[lo, 8 + off:8 + off + CHUNK, :], 0)
        conv_chunk(lambda off: edge_ref[hi, 8 + off:8 + off + CHUNK, :], seq - CHUNK)

        for c in range(1, n_chunks - 1):
            r0 = c * CHUNK
            conv_chunk(lambda off, r0=r0: src_ref[r0 + off:r0 + off + CHUNK, :], r0)

    U, NLF, MPREV, WROW, DECAY = range(N_GATE_ROWS)
    chunk_id = lax.broadcasted_iota(jnp.int32, (n_chunks, LANES), 0)
    for d, (li_ref, lf_ref) in enumerate(((li_f_ref, lf_f_ref), (li_b_ref, lf_b_ref))):
        li = (li_ref[...] + bias_ref[bias_base + d * M_HEADS + head]) * LOG2E
        fpre = lf_ref[...] + bias_ref[bias_base + 2 * M_HEADS + d * M_HEADS + head]
        lf = (jnp.minimum(fpre, 0.0) - jnp.log(1.0 + jnp.exp(-jnp.abs(fpre)))) * LOG2E
        b = _lane_cumsum(lf, reverse=(d == 1))
        u = li - b
        tot = jnp.broadcast_to(jnp.sum(lf, axis=-1, keepdims=True), lf.shape)
        gmax = jnp.broadcast_to(jnp.max(u, axis=-1, keepdims=True), lf.shape) + tot
        m = jnp.full((1, LANES), NEG_INF, F32)
        m_prev = jnp.zeros((n_chunks, LANES), F32)
        m_new = jnp.zeros((n_chunks, LANES), F32)
        for step in range(n_chunks):
            c = step if d == 0 else n_chunks - 1 - step
            m_prev = jnp.where(chunk_id == c, m, m_prev)
            m = jnp.maximum(tot[c:c + 1, :] + m, gmax[c:c + 1, :])
            m_new = jnp.where(chunk_id == c, m, m_new)
        gl_ref[d, U] = u
        gl_ref[d, NLF] = -lf
        gl_ref[d, MPREV] = m_prev
        gl_ref[d, WROW] = jnp.exp2(u + tot - m_new)
        gl_ref[d, DECAY] = jnp.exp2(tot + m_prev - m_new)

    conv_into(q_ref, cwq_ref, qc_ref, 1.0, 0, 1)
    conv_into(k_ref, cwk_ref, kc_ref, 1.0 / math.sqrt(M_HEAD_DIM), 2, 3)

    cst_ref[...] = jnp.zeros(cst_ref.shape, F32)

    def state_body(i, carry):
        for d in range(2):
            c = i if d == 0 else n_chunks - 1 - i
            r0 = pl.multiple_of(c * CHUNK, CHUNK)
            k = kc_ref[pl.ds(r0, CHUNK), :]
            va = va_ref[pl.ds(r0, CHUNK), :]
            w = gl_ref[d, WROW, pl.ds(c, 1), :]
            decay = gl_ref[d, DECAY, pl.ds(c, 1), :]
            upd = jnp.dot((k.T * w).astype(BF16), va, preferred_element_type=F32)
            state = cst_ref[d]
            cprev_ref[d, c] = state.astype(BF16)
            cst_ref[d] = jnp.concatenate([decay, decay], axis=1) * state + upd
        return carry

    for i in range(n_chunks):
        state_body(i, 0)

    row = lax.broadcasted_iota(jnp.int32, (CHUNK, CHUNK), 0)
    col = lax.broadcasted_iota(jnp.int32, (CHUNK, CHUNK), 1)
    masks = (col <= row, col >= row)

    def out_body(c, carry):
        r0 = pl.multiple_of(c * CHUNK, CHUNK)
        q = qc_ref[pl.ds(r0, CHUNK), :]
        k = kc_ref[pl.ds(r0, CHUNK), :]
        va = va_ref[pl.ds(r0, CHUNK), :]
        s = lax.dot_general(q, k.astype(BF16), (((1,), (1,)), ((), ())), preferred_element_type=F32)
        hm = jnp.zeros((CHUNK, LANES), F32)
        for d in range(2):
            mask = masks[d]
            u = gl_ref[d, U, pl.ds(c, 1), :]
            nlf = gl_ref[d, NLF, pl.ds(c, 1), :]
            m_prev = gl_ref[d, MPREV, pl.ds(c, 1), :]
            um = jnp.where(mask, u, NEG_INF)
            mloc = jnp.maximum(jnp.max(um, axis=-1, keepdims=True), m_prev)
            dt = jnp.exp2(um - mloc)
            inter = jnp.exp2(m_prev - mloc)
            lhs = jnp.concatenate([(s * dt).astype(BF16), q * inter.astype(BF16)], axis=1)
            rhs = jnp.concatenate([va, cprev_ref[d, c]], axis=0)
            out = jnp.dot(lhs, rhs, preferred_element_type=F32)
            nb_col = jnp.sum(jnp.where(mask, nlf, 0.0), axis=-1, keepdims=True)
            stab = jnp.exp2(nb_col - mloc)
            hm = hm + out[:, 0:LANES] / jnp.maximum(jnp.abs(out[:, LANES:]), stab)
        ms = jnp.mean(hm * hm, axis=-1, keepdims=True)
        y = hm * lax.rsqrt(ms + NORM_EPS) * hg_ref[...] * o_ref[pl.ds(r0, CHUNK), :]
        y_ref[pl.ds(r0, CHUNK), :] = y.astype(y_ref.dtype)
        return carry

    for c in range(n_chunks):
        out_body(c, 0)


def _mlstm(proj3, va3, gates_t, gate_bias, conv_w, head_norm_g, layer):
    batch, seq, _ = proj3.shape
    n_chunks = seq // CHUNK
    col = lambda off: pl.BlockSpec((None, seq, LANES), lambda b, h: (b, 0, off + h))
    gate = lambda off: pl.BlockSpec((None, None, n_chunks, LANES), lambda b, h: (off + h, b, 0, 0))
    return pl.pallas_call(
        functools.partial(_mlstm_kernel, bias_base=layer * N_GATES),
        grid=(batch, M_HEADS),
        in_specs=[
            pl.BlockSpec(memory_space=pltpu.SMEM),
            col(0), col(M_HEADS),
            pl.BlockSpec((None, seq, 2 * LANES), lambda b, h: (b, 0, h)),
            col(3 * M_HEADS),
            gate(0), gate(M_HEADS), gate(2 * M_HEADS), gate(3 * M_HEADS),
            pl.BlockSpec((None, CONV_WIDTH, LANES), lambda b, h: (layer, 0, h)),
            pl.BlockSpec((None, CONV_WIDTH, LANES), lambda b, h: (layer, 0, M_HEADS + h)),
            pl.BlockSpec((None, 1, LANES), lambda b, h: (layer, 0, h)),
        ],
        out_specs=pl.BlockSpec((None, seq, LANES), lambda b, h: (b, 0, h)),
        out_shape=jax.ShapeDtypeStruct((batch, seq, M_WIDTH), BF16),
        scratch_shapes=[
            pltpu.VMEM((4, CHUNK + 16, LANES), F32),
            pltpu.VMEM((seq, LANES), BF16),
            pltpu.VMEM((seq, LANES), F32),
            pltpu.VMEM((2, M_HEAD_DIM, 2 * LANES), F32),
            pltpu.VMEM((2, n_chunks, M_HEAD_DIM, 2 * LANES), BF16),
            pltpu.VMEM((2, N_GATE_ROWS, n_chunks, LANES), F32),
        ],
        compiler_params=pltpu.CompilerParams(
            dimension_semantics=("arbitrary", "arbitrary"), vmem_limit_bytes=VMEM_LIMIT),
        name="mlstm",
    )(gate_bias, proj3, proj3, va3, proj3, gates_t, gates_t, gates_t, gates_t,
      conv_w, conv_w, head_norm_g)


def _attn_kernel(q_ref, k_ref, v_ref, bias_ref, bias8_ref, y_ref,
                 c4_ref, c8_ref, num_ref, den_ref, mx_ref):
    seq = q_ref.shape[0]
    rope_rows = 256

    d4, d16 = DILATIONS[1], DILATIONS[2]
    d8 = d16 // 2
    len4, len8 = seq // d4, seq // d8

    for r in range(d4):
        for l0 in range(0, len4, rope_rows):
            for t, src in enumerate((q_ref, k_ref, v_ref)):
                c4_ref[t, r * len4 + l0:r * len4 + l0 + rope_rows, :] = src[pl.ds(r + d4 * l0, rope_rows, stride=d4), :]

    for r in range(d8):
        for t in range(3):
            c8_ref[t, r * len8:(r + 1) * len8, :] = c4_ref[t, pl.ds((r % d4) * len4 + r // d4, len8, stride=d8 // d4), :]

    lane_q = lax.broadcasted_iota(jnp.int32, (QB, LANES), 1)
    head0_q = lane_q < A_HEAD_DIM
    head0_qk = (lane_q // (A_HEAD_DIM // 2)) % 2 == 0

    def block(q, k, v, bias):
        span = k.shape[0]
        q2 = jnp.concatenate([jnp.where(head0_qk, q, 0.0), jnp.where(head0_qk, 0.0, q)], axis=0).astype(BF16)
        s = lax.dot_general(q2, k.astype(BF16), (((1,), (1,)), ((), ())), preferred_element_type=F32)
        s = s + jnp.concatenate([bias, bias], axis=0)
        m = jnp.max(s, axis=-1, keepdims=True)
        p = jnp.exp2(s - m).astype(BF16)
        rhs = jnp.concatenate([v.astype(BF16), jnp.ones((span, LANES), BF16)], axis=1)
        acc = jnp.dot(p, rhs, preferred_element_type=F32)
        return (jnp.where(head0_q, acc[0:QB, 0:LANES], acc[QB:, 0:LANES]),
                jnp.where(head0_q, acc[0:QB, LANES:], acc[QB:, LANES:]),
                jnp.where(head0_q, m[0:QB], m[QB:2 * QB]))

    stats_refs = (num_ref, den_ref, mx_ref)

    nat = (lambda rows: q_ref[rows, :], lambda rows: k_ref[rows, :], lambda rows: v_ref[rows, :])
    cm4 = tuple((lambda rows, t=t: c4_ref[t, rows, :]) for t in range(3))
    cm8 = tuple((lambda rows, t=t: c8_ref[t, rows, :]) for t in range(3))
    patterns = ((DILATIONS[0], HALF_WINDOW, nat, bias_ref),
                (d4, HALF_WINDOW, cm4, bias_ref),
                (d8, HALF_WINDOW * (d16 // d8), cm8, bias8_ref))

    def band_block(pat, r, j):
        mod, half, srcs, biases = patterns[pat]
        length = seq // mod
        q0 = j * QB
        k0 = min(max(q0 - half, 0), length - SPAN)
        qrows = pl.ds(r * length + q0, QB)
        krows = pl.ds(r * length + k0, SPAN)
        out_rows = qrows if mod == 1 else pl.ds(r + mod * q0, QB, stride=mod)
        res = block(srcs[0](qrows), srcs[1](krows), srcs[2](krows), biases[(q0 - k0) // half])
        for ref, val in zip(stats_refs, res):
            ref[pat, out_rows, :] = val

    def merge(c):
        rows = pl.ds(c * rope_rows, rope_rows)
        m_all = jnp.maximum(jnp.maximum(mx_ref[0, rows, :], mx_ref[1, rows, :]), mx_ref[2, rows, :])
        num = jnp.zeros((rope_rows, LANES), F32)
        den = jnp.zeros((rope_rows, LANES), F32)
        for pat in range(len(DILATIONS)):
            w = jnp.exp2(mx_ref[pat, rows, :] - m_all)
            num = num + w * num_ref[pat, rows, :]
            den = den + w * den_ref[pat, rows, :]
        y_ref[rows, :] = (num / den).astype(y_ref.dtype)

    n_groups = len8 // QB
    group_rows = seq // n_groups
    for g in range(n_groups):
        for pat in reversed(range(len(patterns))):
            mod = patterns[pat][0]
            per_group = (seq // mod // QB) // n_groups
            for r in range(mod):
                for j in range(g * per_group, (g + 1) * per_group):
                    band_block(pat, r, j)
        for c in range(g * group_rows // rope_rows, (g + 1) * group_rows // rope_rows):
            merge(c)


def _attention(proj3, bias, bias8):
    batch, seq, _ = proj3.shape
    assert DILATIONS == (1, 4, 16) and seq * 2 // DILATIONS[2] == SPAN
    n_pairs = A_WIDTH // LANES
    base = 4 * M_WIDTH // LANES
    col = lambda off: pl.BlockSpec((None, seq, LANES), lambda b, p: (b, 0, base + off + p))
    const2 = lambda shape: pl.BlockSpec(shape, lambda b, p: (0,) * len(shape))
    n_pat = len(DILATIONS)
    return pl.pallas_call(
        _attn_kernel,
        grid=(batch, n_pairs),
        in_specs=[
            col(0), col(n_pairs), col(2 * n_pairs),
            const2((3, QB, SPAN)), const2((2, QB, SPAN)),
        ],
        out_specs=pl.BlockSpec((None, seq, LANES), lambda b, p: (b, 0, p)),
        out_shape=jax.ShapeDtypeStruct((batch, seq, A_WIDTH), BF16),
        scratch_shapes=[
            pltpu.VMEM((3, seq, LANES), F32),
            pltpu.VMEM((3, seq, LANES), F32),
            pltpu.VMEM((n_pat, seq, LANES), F32),
            pltpu.VMEM((n_pat, seq, LANES), F32),
            pltpu.VMEM((n_pat, seq, LANES), F32),
        ],
        compiler_params=pltpu.CompilerParams(
            dimension_semantics=("arbitrary", "arbitrary"), vmem_limit_bytes=VMEM_LIMIT),
        name="dilated_attn",
    )(proj3, proj3, proj3, bias, bias8)


def _out_mlp_kernel(x_ref, ym_ref, ya_ref, wo_ref, g2_ref, wu_ref, wd_ref, fg_ref, o_ref, usq_ref,
                    *, ff_chunk, final):
    x1 = (x_ref[...]
          + jnp.dot(ym_ref[...], wo_ref[0:M_WIDTH, :], preferred_element_type=F32)
          + jnp.dot(ya_ref[...], wo_ref[M_WIDTH:, :], preferred_element_type=F32))
    ms = jnp.mean(x1 * x1, axis=-1, keepdims=True)
    h = (x1 * lax.rsqrt(ms + NORM_EPS) * g2_ref[...]).astype(BF16)
    for c in range(0, D_FF, ff_chunk):
        u = jnp.maximum(jnp.dot(h, wu_ref[:, c:c + ff_chunk], preferred_element_type=F32), 0.0)
        usq_ref[:, c:c + ff_chunk] = (u * u).astype(BF16)
    out = x1 + jnp.dot(usq_ref[...], wd_ref[...], preferred_element_type=F32)
    if final:
        ms = jnp.mean(out * out, axis=-1, keepdims=True)
        out = out * lax.rsqrt(ms + NORM_EPS) * fg_ref[...]
    o_ref[...] = out


def _out_mlp(x2d, ym, ya, w_out, g2, w_up, w_down, final_g, layer, final, tm=512):
    m = x2d.shape[0]
    resident = lambda shape: pl.BlockSpec((None,) + shape, lambda i: (layer, 0, 0), pipeline_mode=pl.Buffered(1))
    return pl.pallas_call(
        functools.partial(_out_mlp_kernel, ff_chunk=1024, final=final),
        grid=(m // tm,),
        in_specs=[
            pl.BlockSpec((tm, D_MODEL), lambda i: (i, 0)),
            pl.BlockSpec((tm, M_WIDTH), lambda i: (i, 0)),
            pl.BlockSpec((tm, A_WIDTH), lambda i: (i, 0)),
            resident((D_MODEL, D_MODEL)),
            resident((1, D_MODEL)),
            resident((D_MODEL, D_FF)),
            resident((D_FF, D_MODEL)),
            pl.BlockSpec((1, D_MODEL), lambda i: (0, 0), pipeline_mode=pl.Buffered(1)),
        ],
        out_specs=pl.BlockSpec((tm, D_MODEL), lambda i: (i, 0)),
        out_shape=jax.ShapeDtypeStruct((m, D_MODEL), F32),
        scratch_shapes=[pltpu.VMEM((tm, D_FF), BF16)],
        compiler_params=pltpu.CompilerParams(
            dimension_semantics=("arbitrary",), vmem_limit_bytes=VMEM_LIMIT),
        name="out_mlp",
    )(x2d, ym, ya, w_out, g2, w_up, w_down, final_g)


def _rope_tables(seq):
    half = A_HEAD_DIM // 2
    inv_freq = ROPE_THETA ** (-jnp.arange(half, dtype=F32) / half)
    ang = jnp.arange(seq, dtype=F32)[:, None] * inv_freq[None, :]
    cos = jnp.tile(jnp.cos(ang), (1, LANES // half))
    sin = jnp.sin(ang)
    sin = jnp.concatenate([-sin, -sin, sin, sin], axis=1)
    q_scale = LOG2E / math.sqrt(A_HEAD_DIM)
    return jnp.concatenate([cos * q_scale, sin * q_scale, cos, sin], axis=1)


def _band_bias():
    row = jnp.arange(QB)[:, None]
    col = jnp.arange(SPAN)[None, :]
    variants = []
    for off in (0, HALF_WINDOW, 2 * HALF_WINDOW):
        rel = col - off - row
        variants.append(jnp.where(jnp.abs(rel) <= HALF_WINDOW, 0.0, NEG_INF))
    variants8 = []
    for off in (0, QB):
        rel = col - off - row
        variants8.append(jnp.where((jnp.abs(rel) <= 2 * HALF_WINDOW) & (rel % 2 == 0), 0.0, NEG_INF))
    return jnp.stack(variants).astype(F32), jnp.stack(variants8).astype(F32)


def kernel(x, norm1_g, w_in, conv_w, gate_i_b, gate_f_b, head_norm_g, w_out, norm2_g, w_up, w_down, final_g):
    batch, seq, _ = x.shape
    depth = w_in.shape[0]
    n_chunks = seq // CHUNK
    rope_tab = _rope_tables(seq)
    bias, bias8 = _band_bias()
    gate_lo = 4 * M_WIDTH
    attn_lo = gate_lo + N_GATES

    half = A_HEAD_DIM // 2
    w_qk = w_in[:, :, attn_lo:attn_lo + 2 * A_WIDTH].reshape(depth, D_MODEL, -1, 2, 2, half)
    w_qk = jnp.swapaxes(w_qk, 3, 4).reshape(depth, D_MODEL, 2 * A_WIDTH)
    w_cat = jnp.concatenate(
        [w_in[:, :, :gate_lo], w_qk, w_in[:, :, attn_lo + 2 * A_WIDTH:],
         w_in[:, :, gate_lo:attn_lo], jnp.zeros((depth, D_MODEL, LANES - N_GATES), F32)], axis=2).astype(BF16)
    w_out_b, w_up_b, w_down_b = w_out.astype(BF16), w_up.astype(BF16), w_down.astype(BF16)
    gate_bias = jnp.concatenate([gate_i_b, gate_f_b], axis=1).reshape(depth * N_GATES).astype(F32)
    norm1_3, norm2_3, head_g3 = norm1_g[:, None, :], norm2_g[:, None, :], head_norm_g[:, None, :]

    x2d = x.reshape(batch * seq, D_MODEL)
    for layer in range(depth):
        proj, va, gates_t = _in_proj(x2d, norm1_3, w_cat, rope_tab, layer, seq)
        proj3 = proj.reshape(batch, seq, PROJ_COLS)
        ym = _mlstm(proj3, va.reshape(batch, seq, 2 * M_WIDTH), gates_t.reshape(LANES, batch, n_chunks, CHUNK),
                    gate_bias, conv_w, head_g3, layer)
        ya = _attention(proj3, bias, bias8)
        x2d = _out_mlp(x2d, ym.reshape(batch * seq, M_WIDTH), ya.reshape(batch * seq, A_WIDTH),
                       w_out_b, norm2_3, w_up_b, w_down_b, final_g[None, :],
                       layer, final=(layer == depth - 1))
    return x2d.reshape(batch, seq, D_MODEL)
```

```python
import functools
import math

import jax
import jax.numpy as jnp
from jax import lax
from jax.experimental import pallas as pl
from jax.experimental.pallas import tpu as pltpu

F32 = jnp.float32
BF16 = jnp.bfloat16

D_MODEL = 1024
M_HEADS = 4
M_HEAD_DIM = 128
M_WIDTH = M_HEADS * M_HEAD_DIM
A_HEADS = 8
A_HEAD_DIM = 64
A_WIDTH = A_HEADS * A_HEAD_DIM
D_FF = 4 * D_MODEL
CONV_WIDTH = 5
ROPE_THETA = 10000.0
NORM_EPS = 1e-6
NEG_INF = -1e30
DILATIONS = (1, 4, 16)
HALF_WINDOW = 64

LANES = 128
N_GATES = 4 * M_HEADS
MAIN_COLS = 4 * M_WIDTH + 3 * A_WIDTH
PROJ_COLS = MAIN_COLS + LANES
MV_COL = 2 * M_WIDTH
OGATE_COL = 3 * M_WIDTH
AQ_COL = 4 * M_WIDTH
AK_COL = AQ_COL + A_WIDTH
CHUNK = 128
QB = 128
SPAN = QB + 2 * HALF_WINDOW
VMEM_LIMIT = 56 * 1024 * 1024
LOG2E = 1.4426950408889634
N_GATE_ROWS = 5


def _sigmoid(x):
    return 1.0 / (1.0 + jnp.exp(-x))


def _in_proj_kernel(x_ref, g_ref, wm_ref, wqk_ref, wv_ref, wg_ref, rope_ref, o_ref, va_ref, gt_ref):
    x = x_ref[...]
    ms = jnp.mean(x * x, axis=-1, keepdims=True)
    h = (x * lax.rsqrt(ms + NORM_EPS) * g_ref[...]).astype(BF16)
    groups = ((0, wm_ref), (AQ_COL, wqk_ref), (AQ_COL + 2 * A_WIDTH, wv_ref), (MAIN_COLS, wg_ref))
    for c in range(0, PROJ_COLS, M_WIDTH):
        w = min(M_WIDTH, PROJ_COLS - c)
        base, w_ref = [(b0, ref) for b0, ref in groups if b0 <= c][-1]
        r = jnp.dot(h, w_ref[:, c - base:c - base + w], preferred_element_type=F32)
        if c == OGATE_COL:
            r = _sigmoid(r)
        if c == MV_COL:
            for hd in range(M_HEADS):
                va_ref[:, 2 * hd * LANES:(2 * hd + 1) * LANES] = r[:, hd * LANES:(hd + 1) * LANES].astype(BF16)
                va_ref[:, (2 * hd + 1) * LANES:(2 * hd + 2) * LANES] = jnp.ones((r.shape[0], LANES), BF16)
        if c == MAIN_COLS:
            gt_ref[...] = r.T
        if c in (AQ_COL, AK_COL):
            t = 0 if c == AQ_COL else 2
            cos = rope_ref[:, t * LANES:(t + 1) * LANES]
            sin = rope_ref[:, (t + 1) * LANES:(t + 2) * LANES]
            for j in range(0, w, LANES):
                xj = r[:, j:j + LANES]
                o_ref[:, c + j:c + j + LANES] = xj * cos + pltpu.roll(xj, LANES // 2, axis=1) * sin
        else:
            o_ref[:, c:c + w] = r


def _in_proj(x2d, g, weights, rope_tab, layer, seq, tm=512):
    m = x2d.shape[0]
    tiles_per_seq = seq // tm
    resident = lambda a: pl.BlockSpec((None,) + a.shape[1:], lambda i: (layer, 0, 0), pipeline_mode=pl.Buffered(1))
    return pl.pallas_call(
        _in_proj_kernel,
        grid=(m // tm,),
        in_specs=[
            pl.BlockSpec((tm, D_MODEL), lambda i: (i, 0)),
            pl.BlockSpec((None, 1, D_MODEL), lambda i: (layer, 0, 0)),
            *[resident(a) for a in weights],
            pl.BlockSpec((tm, 4 * LANES), lambda i: (i % tiles_per_seq, 0)),
        ],
        out_specs=[
            pl.BlockSpec((tm, PROJ_COLS), lambda i: (i, 0)),
            pl.BlockSpec((tm, 2 * M_WIDTH), lambda i: (i, 0)),
            pl.BlockSpec((LANES, tm), lambda i: (0, i)),
        ],
        out_shape=[
            jax.ShapeDtypeStruct((m, PROJ_COLS), F32),
            jax.ShapeDtypeStruct((m, 2 * M_WIDTH), BF16),
            jax.ShapeDtypeStruct((LANES, m), F32),
        ],
        compiler_params=pltpu.CompilerParams(
            dimension_semantics=("arbitrary",), vmem_limit_bytes=VMEM_LIMIT),
        name="in_proj",
    )(x2d, g, *weights, rope_tab)


def _lane_cumsum(x, reverse):
    lane = lax.broadcasted_iota(jnp.int32, x.shape, 1)
    k = 1
    while k < LANES:
        if reverse:
            x = x + jnp.where(lane < LANES - k, pltpu.roll(x, LANES - k, axis=1), 0.0)
        else:
            x = x + jnp.where(lane >= k, pltpu.roll(x, k, axis=1), 0.0)
        k *= 2
    return x


def _mlstm_kernel(bias_ref, q_ref, k_ref, va_ref, o_ref, li_f_ref, li_b_ref, lf_f_ref, lf_b_ref,
                  cwq_ref, cwk_ref, hg_ref, y_ref,
                  edge_ref, qc_ref, kc_ref, cst_ref, cprev_ref, gl_ref, *, bias_base):
    seq = q_ref.shape[0]
    n_chunks = seq // CHUNK
    head = pl.program_id(1)
    pad = CONV_WIDTH // 2

    U, NLF, MPREV, WROW, DECAY = range(N_GATE_ROWS)
    chunk_id = lax.broadcasted_iota(jnp.int32, (n_chunks, LANES), 0)
    for d, (li_ref, lf_ref) in enumerate(((li_f_ref, lf_f_ref), (li_b_ref, lf_b_ref))):
        li = (li_ref[...] + bias_ref[bias_base + d * M_HEADS + head]) * LOG2E
        fpre = lf_ref[...] + bias_ref[bias_base + 2 * M_HEADS + d * M_HEADS + head]
        lf = (jnp.minimum(fpre, 0.0) - jnp.log(1.0 + jnp.exp(-jnp.abs(fpre)))) * LOG2E
        b = _lane_cumsum(lf, reverse=(d == 1))
        u = li - b
        tot = jnp.broadcast_to(jnp.sum(lf, axis=-1, keepdims=True), lf.shape)
        gmax = jnp.broadcast_to(jnp.max(u, axis=-1, keepdims=True), lf.shape) + tot
        m = jnp.full((1, LANES), NEG_INF, F32)
        m_prev = jnp.zeros((n_chunks, LANES), F32)
        m_new = jnp.zeros((n_chunks, LANES), F32)
        for step in range(n_chunks):
            c = step if d == 0 else n_chunks - 1 - step
            m_prev = jnp.where(chunk_id == c, m, m_prev)
            m = jnp.maximum(tot[c:c + 1, :] + m, gmax[c:c + 1, :])
            m_new = jnp.where(chunk_id == c, m, m_new)
        gl_ref[d, U] = u
        gl_ref[d, NLF] = -lf
        gl_ref[d, MPREV] = m_prev
        gl_ref[d, WROW] = jnp.exp2(u + tot - m_new)
        gl_ref[d, DECAY] = jnp.exp2(tot + m_prev - m_new)

    def conv_chunk(c, src_ref, cw_ref, dst_ref, scale, slot):
        r0 = c * CHUNK
        if c == 0:
            edge_ref[slot, 0:8, :] = jnp.zeros((8, LANES), F32)
            edge_ref[slot, 8:8 + CHUNK + 8, :] = src_ref[0:CHUNK + 8, :]
            window = lambda off: edge_ref[slot, 8 + off:8 + off + CHUNK, :]
        elif c == n_chunks - 1:
            edge_ref[slot + 1, 0:CHUNK + 8, :] = src_ref[seq - CHUNK - 8:seq, :]
            edge_ref[slot + 1, CHUNK + 8:CHUNK + 16, :] = jnp.zeros((8, LANES), F32)
            window = lambda off: edge_ref[slot + 1, 8 + off:8 + off + CHUNK, :]
        else:
            window = lambda off: src_ref[r0 + off:r0 + off + CHUNK, :]
        acc = jnp.zeros((CHUNK, LANES), F32)
        for t in range(CONV_WIDTH):
            acc = acc + cw_ref[t:t + 1, :] * window(t - pad)
        act = acc * _sigmoid(acc)
        if scale != 1.0:
            act = act * scale
        dst_ref[r0:r0 + CHUNK, :] = act.astype(dst_ref.dtype)

    q_conv = lambda c: conv_chunk(c, q_ref, cwq_ref, qc_ref, 1.0, 0)
    k_conv = lambda c: conv_chunk(c, k_ref, cwk_ref, kc_ref, 1.0 / math.sqrt(M_HEAD_DIM), 2)

    cst_ref[...] = jnp.zeros(cst_ref.shape, F32)

    def state_step(i):
        for d in range(2):
            c = i if d == 0 else n_chunks - 1 - i
            rows = pl.ds(c * CHUNK, CHUNK)
            w = gl_ref[d, WROW, c:c + 1, :]
            decay = gl_ref[d, DECAY, c:c + 1, :]
            upd = jnp.dot((kc_ref[rows, :].T * w).astype(BF16), va_ref[rows, :],
                          preferred_element_type=F32)
            state = cst_ref[d]
            cprev_ref[d, c] = state.astype(BF16)
            cst_ref[d] = jnp.concatenate([decay, decay], axis=1) * state + upd

    row = lax.broadcasted_iota(jnp.int32, (CHUNK, CHUNK), 0)
    col = lax.broadcasted_iota(jnp.int32, (CHUNK, CHUNK), 1)
    masks = (col <= row, col >= row)

    def out_chunk(c):
        rows = pl.ds(c * CHUNK, CHUNK)
        q = qc_ref[rows, :]
        va = va_ref[rows, :]
        s = lax.dot_general(q, kc_ref[rows, :].astype(BF16), (((1,), (1,)), ((), ())), preferred_element_type=F32)
        hm = jnp.zeros((CHUNK, LANES), F32)
        for d in range(2):
            mask = masks[d]
            u = gl_ref[d, U, c:c + 1, :]
            nlf = gl_ref[d, NLF, c:c + 1, :]
            m_prev = gl_ref[d, MPREV, c:c + 1, :]
            um = jnp.where(mask, u, NEG_INF)
            mloc = jnp.maximum(jnp.max(um, axis=-1, keepdims=True), m_prev)
            dt = jnp.exp2(um - mloc)
            inter = jnp.exp2(m_prev - mloc)
            lhs = jnp.concatenate([(s * dt).astype(BF16), q * inter.astype(BF16)], axis=1)
            rhs = jnp.concatenate([va, cprev_ref[d, c]], axis=0)
            out = jnp.dot(lhs, rhs, preferred_element_type=F32)
            nb_col = jnp.sum(jnp.where(mask, nlf, 0.0), axis=-1, keepdims=True)
            stab = jnp.exp2(nb_col - mloc)
            hm = hm + out[:, 0:LANES] / jnp.maximum(jnp.abs(out[:, LANES:]), stab)
        ms = jnp.mean(hm * hm, axis=-1, keepdims=True)
        y = hm * lax.rsqrt(ms + NORM_EPS) * hg_ref[...] * o_ref[rows, :]
        y_ref[rows, :] = y.astype(y_ref.dtype)

    for conv in (q_conv, k_conv):
        for c in (0, n_chunks - 1, *range(1, n_chunks - 1)):
            conv(c)
    for i in range(n_chunks):
        state_step(i)
    for c in range(n_chunks):
        out_chunk(c)


def _mlstm(proj3, va3, gates_t, gate_bias, conv_w, head_norm_g, layer):
    batch, seq, _ = proj3.shape
    n_chunks = seq // CHUNK
    col = lambda off: pl.BlockSpec((None, seq, LANES), lambda b, h: (b, 0, off + h))
    gate = lambda off: pl.BlockSpec((None, None, n_chunks, LANES), lambda b, h: (off + h, b, 0, 0))
    return pl.pallas_call(
        functools.partial(_mlstm_kernel, bias_base=layer * N_GATES),
        grid=(batch, M_HEADS),
        in_specs=[
            pl.BlockSpec(memory_space=pltpu.SMEM),
            col(0), col(M_HEADS),
            pl.BlockSpec((None, seq, 2 * LANES), lambda b, h: (b, 0, h)),
            col(OGATE_COL // LANES),
            gate(0), gate(M_HEADS), gate(2 * M_HEADS), gate(3 * M_HEADS),
            pl.BlockSpec((None, CONV_WIDTH, LANES), lambda b, h: (layer, 0, h)),
            pl.BlockSpec((None, CONV_WIDTH, LANES), lambda b, h: (layer, 0, M_HEADS + h)),
            pl.BlockSpec((None, 1, LANES), lambda b, h: (layer, 0, h)),
        ],
        out_specs=pl.BlockSpec((None, seq, LANES), lambda b, h: (b, 0, h)),
        out_shape=jax.ShapeDtypeStruct((batch, seq, M_WIDTH), BF16),
        scratch_shapes=[
            pltpu.VMEM((4, CHUNK + 16, LANES), F32),
            pltpu.VMEM((seq, LANES), BF16),
            pltpu.VMEM((seq, LANES), F32),
            pltpu.VMEM((2, M_HEAD_DIM, 2 * LANES), F32),
            pltpu.VMEM((2, n_chunks, M_HEAD_DIM, 2 * LANES), BF16),
            pltpu.VMEM((2, N_GATE_ROWS, n_chunks, LANES), F32),
        ],
        compiler_params=pltpu.CompilerParams(
            dimension_semantics=("arbitrary", "arbitrary"), vmem_limit_bytes=VMEM_LIMIT),
        name="mlstm",
    )(gate_bias, proj3, proj3, va3, proj3, gates_t, gates_t, gates_t, gates_t,
      conv_w, conv_w, head_norm_g)


def _attn_kernel(q_ref, k_ref, v_ref, bias_ref, bias8_ref, y_ref,
                 c4_ref, c8_ref, num_ref, den_ref, mx_ref):
    seq = q_ref.shape[0]
    rope_rows = 256

    d4, d16 = DILATIONS[1], DILATIONS[2]
    d8 = d16 // 2
    len4, len8 = seq // d4, seq // d8

    def split4(r, l0):
        for t, src in enumerate((q_ref, k_ref, v_ref)):
            c4_ref[t, r * len4 + l0:r * len4 + l0 + rope_rows, :] = src[pl.ds(r + d4 * l0, rope_rows, stride=d4), :]

    def split8(r):
        for t in range(3):
            c8_ref[t, r * len8:(r + 1) * len8, :] = c4_ref[t, pl.ds((r % d4) * len4 + r // d4, len8, stride=d8 // d4), :]

    lane_q = lax.broadcasted_iota(jnp.int32, (QB, LANES), 1)
    head0_q = lane_q < A_HEAD_DIM
    head0_qk = (lane_q // (A_HEAD_DIM // 2)) % 2 == 0

    def block(q, k, v, bias):
        span = k.shape[0]
        q2 = jnp.concatenate([jnp.where(head0_qk, q, 0.0), jnp.where(head0_qk, 0.0, q)], axis=0).astype(BF16)
        s = lax.dot_general(q2, k.astype(BF16), (((1,), (1,)), ((), ())), preferred_element_type=F32)
        s = s + jnp.concatenate([bias, bias], axis=0)
        m = jnp.max(s, axis=-1, keepdims=True)
        p = jnp.exp2(s - m).astype(BF16)
        rhs = jnp.concatenate([v.astype(BF16), jnp.ones((span, LANES), BF16)], axis=1)
        acc = jnp.dot(p, rhs, preferred_element_type=F32)
        return (jnp.where(head0_q, acc[0:QB, 0:LANES], acc[QB:, 0:LANES]),
                jnp.where(head0_q, acc[0:QB, LANES:], acc[QB:, LANES:]),
                jnp.where(head0_q, m[0:QB], m[QB:2 * QB]))

    stats_refs = (num_ref, den_ref, mx_ref)

    nat = (lambda rows: q_ref[rows, :], lambda rows: k_ref[rows, :], lambda rows: v_ref[rows, :])
    cm4 = tuple((lambda rows, t=t: c4_ref[t, rows, :]) for t in range(3))
    cm8 = tuple((lambda rows, t=t: c8_ref[t, rows, :]) for t in range(3))
    patterns = ((DILATIONS[0], HALF_WINDOW, nat, bias_ref),
                (d4, HALF_WINDOW, cm4, bias_ref),
                (d8, HALF_WINDOW * (d16 // d8), cm8, bias8_ref))

    def band_block(pat, r, j):
        mod, half, srcs, biases = patterns[pat]
        length = seq // mod
        q0 = j * QB
        k0 = min(max(q0 - half, 0), length - SPAN)
        qrows = pl.ds(r * length + q0, QB)
        krows = pl.ds(r * length + k0, SPAN)
        out_rows = qrows if mod == 1 else pl.ds(r + mod * q0, QB, stride=mod)
        res = block(srcs[0](qrows), srcs[1](krows), srcs[2](krows), biases[(q0 - k0) // half])
        for ref, val in zip(stats_refs, res):
            ref[pat, out_rows, :] = val

    def merge(c):
        rows = pl.ds(c * rope_rows, rope_rows)
        m_all = jnp.maximum(jnp.maximum(mx_ref[0, rows, :], mx_ref[1, rows, :]), mx_ref[2, rows, :])
        num = jnp.zeros((rope_rows, LANES), F32)
        den = jnp.zeros((rope_rows, LANES), F32)
        for pat in range(len(DILATIONS)):
            w = jnp.exp2(mx_ref[pat, rows, :] - m_all)
            num = num + w * num_ref[pat, rows, :]
            den = den + w * den_ref[pat, rows, :]
        y_ref[rows, :] = (num / den).astype(y_ref.dtype)

    for r in range(d4):
        for l0 in range(0, len4, rope_rows):
            split4(r, l0)
    for r in range(d8):
        split8(r)
    n_groups = len8 // QB
    group_rows = seq // n_groups
    for g in range(n_groups):
        for pat in reversed(range(len(patterns))):
            mod = patterns[pat][0]
            per_group = (seq // mod // QB) // n_groups
            for r in range(mod):
                for j in range(g * per_group, (g + 1) * per_group):
                    band_block(pat, r, j)
        for c in range(g * group_rows // rope_rows, (g + 1) * group_rows // rope_rows):
            merge(c)


def _attention(proj3, bias, bias8):
    batch, seq, _ = proj3.shape
    assert DILATIONS == (1, 4, 16) and seq * 2 // DILATIONS[2] == SPAN
    n_pairs = A_WIDTH // LANES
    base = 4 * M_WIDTH // LANES
    col = lambda off: pl.BlockSpec((None, seq, LANES), lambda b, p: (b, 0, base + off + p))
    const2 = lambda shape: pl.BlockSpec(shape, lambda b, p: (0,) * len(shape))
    n_pat = len(DILATIONS)
    return pl.pallas_call(
        _attn_kernel,
        grid=(batch, n_pairs),
        in_specs=[
            col(0), col(n_pairs), col(2 * n_pairs),
            const2((3, QB, SPAN)), const2((2, QB, SPAN)),
        ],
        out_specs=pl.BlockSpec((None, seq, LANES), lambda b, p: (b, 0, p)),
        out_shape=jax.ShapeDtypeStruct((batch, seq, A_WIDTH), BF16),
        scratch_shapes=[
            pltpu.VMEM((3, seq, LANES), F32),
            pltpu.VMEM((3, seq, LANES), F32),
            pltpu.VMEM((n_pat, seq, LANES), F32),
            pltpu.VMEM((n_pat, seq, LANES), F32),
            pltpu.VMEM((n_pat, seq, LANES), F32),
        ],
        compiler_params=pltpu.CompilerParams(
            dimension_semantics=("arbitrary", "arbitrary"), vmem_limit_bytes=VMEM_LIMIT),
        name="dilated_attn",
    )(proj3, proj3, proj3, bias, bias8)


def _out_mlp_kernel(x_ref, ym_ref, ya_ref, wo_ref, g2_ref, wu_ref, wd_ref, fg_ref, o_ref, usq_ref,
                    *, ff_chunk, final):
    x1 = (x_ref[...]
          + jnp.dot(ym_ref[...], wo_ref[0:M_WIDTH, :], preferred_element_type=F32)
          + jnp.dot(ya_ref[...], wo_ref[M_WIDTH:, :], preferred_element_type=F32))
    ms = jnp.mean(x1 * x1, axis=-1, keepdims=True)
    h = (x1 * lax.rsqrt(ms + NORM_EPS) * g2_ref[...]).astype(BF16)
    for c in range(0, D_FF, ff_chunk):
        u = jnp.maximum(jnp.dot(h, wu_ref[:, c:c + ff_chunk], preferred_element_type=F32), 0.0)
        usq_ref[:, c:c + ff_chunk] = (u * u).astype(BF16)
    out = x1 + jnp.dot(usq_ref[...], wd_ref[...], preferred_element_type=F32)
    if final:
        ms = jnp.mean(out * out, axis=-1, keepdims=True)
        out = out * lax.rsqrt(ms + NORM_EPS) * fg_ref[...]
    o_ref[...] = out


def _out_mlp(x2d, ym, ya, w_out, g2, w_up, w_down, final_g, layer, final, tm=512):
    m = x2d.shape[0]
    resident = lambda shape: pl.BlockSpec((None,) + shape, lambda i: (layer, 0, 0), pipeline_mode=pl.Buffered(1))
    return pl.pallas_call(
        functools.partial(_out_mlp_kernel, ff_chunk=1024, final=final),
        grid=(m // tm,),
        in_specs=[
            pl.BlockSpec((tm, D_MODEL), lambda i: (i, 0)),
            pl.BlockSpec((tm, M_WIDTH), lambda i: (i, 0)),
            pl.BlockSpec((tm, A_WIDTH), lambda i: (i, 0)),
            resident((D_MODEL, D_MODEL)),
            resident((1, D_MODEL)),
            resident((D_MODEL, D_FF)),
            resident((D_FF, D_MODEL)),
            pl.BlockSpec((1, D_MODEL), lambda i: (0, 0), pipeline_mode=pl.Buffered(1)),
        ],
        out_specs=pl.BlockSpec((tm, D_MODEL), lambda i: (i, 0)),
        out_shape=jax.ShapeDtypeStruct((m, D_MODEL), F32),
        scratch_shapes=[pltpu.VMEM((tm, D_FF), BF16)],
        compiler_params=pltpu.CompilerParams(
            dimension_semantics=("arbitrary",), vmem_limit_bytes=VMEM_LIMIT),
        name="out_mlp",
    )(x2d, ym, ya, w_out, g2, w_up, w_down, final_g)


def _rope_tables(seq):
    half = A_HEAD_DIM // 2
    inv_freq = ROPE_THETA ** (-jnp.arange(half, dtype=F32) / half)
    ang = jnp.arange(seq, dtype=F32)[:, None] * inv_freq[None, :]
    cos = jnp.tile(jnp.cos(ang), (1, LANES // half))
    sin = jnp.sin(ang)
    sin = jnp.concatenate([-sin, -sin, sin, sin], axis=1)
    q_scale = LOG2E / math.sqrt(A_HEAD_DIM)
    return jnp.concatenate([cos * q_scale, sin * q_scale, cos, sin], axis=1)


def _band_bias():
    row = jnp.arange(QB)[:, None]
    col = jnp.arange(SPAN)[None, :]
    variants = []
    for off in (0, HALF_WINDOW, 2 * HALF_WINDOW):
        rel = col - off - row
        variants.append(jnp.where(jnp.abs(rel) <= HALF_WINDOW, 0.0, NEG_INF))
    variants8 = []
    for off in (0, QB):
        rel = col - off - row
        variants8.append(jnp.where((jnp.abs(rel) <= 2 * HALF_WINDOW) & (rel % 2 == 0), 0.0, NEG_INF))
    return jnp.stack(variants).astype(F32), jnp.stack(variants8).astype(F32)


def kernel(x, norm1_g, w_in, conv_w, gate_i_b, gate_f_b, head_norm_g, w_out, norm2_g, w_up, w_down, final_g):
    batch, seq, _ = x.shape
    depth = w_in.shape[0]
    n_chunks = seq // CHUNK
    rope_tab = _rope_tables(seq)
    bias, bias8 = _band_bias()
    gate_lo = 4 * M_WIDTH
    attn_lo = gate_lo + N_GATES

    half = A_HEAD_DIM // 2
    w_qk = w_in[:, :, attn_lo:attn_lo + 2 * A_WIDTH].reshape(depth, D_MODEL, -1, 2, 2, half)
    w_qk = jnp.swapaxes(w_qk, 3, 4).reshape(depth, D_MODEL, 2 * A_WIDTH)
    w_gates = jnp.pad(w_in[:, :, gate_lo:attn_lo], ((0, 0), (0, 0), (0, LANES - N_GATES)))
    w_groups = tuple(a.astype(BF16) for a in (
        w_in[:, :, :gate_lo], w_qk, w_in[:, :, attn_lo + 2 * A_WIDTH:], w_gates))
    w_out_b, w_up_b, w_down_b = w_out.astype(BF16), w_up.astype(BF16), w_down.astype(BF16)
    gate_bias = jnp.concatenate([gate_i_b, gate_f_b], axis=1).reshape(depth * N_GATES).astype(F32)
    norm1_3, norm2_3, head_g3 = norm1_g[:, None, :], norm2_g[:, None, :], head_norm_g[:, None, :]

    x2d = x.reshape(batch * seq, D_MODEL)
    for layer in range(depth):
        proj, va, gates_t = _in_proj(x2d, norm1_3, w_groups, rope_tab, layer, seq)
        proj3 = proj.reshape(batch, seq, PROJ_COLS)
        ym = _mlstm(proj3, va.reshape(batch, seq, 2 * M_WIDTH), gates_t.reshape(LANES, batch, n_chunks, CHUNK),
                    gate_bias, conv_w, head_g3, layer)
        ya = _attention(proj3, bias, bias8)
        x2d = _out_mlp(x2d, ym.reshape(batch * seq, M_WIDTH), ya.reshape(batch * seq, A_WIDTH),
                       w_out_b, norm2_3, w_up_b, w_down_b, final_g[None, :],
                       layer, final=(layer == depth - 1))
    return x2d.reshape(batch, seq, D_MODEL)
```

```python
import functools
import math

import jax
import jax.numpy as jnp
import numpy as np
from jax import lax
from jax.experimental import pallas as pl
from jax.experimental.pallas import tpu as pltpu

F32 = jnp.float32
BF16 = jnp.bfloat16

D_MODEL = 1024
M_HEADS = 4
M_HEAD_DIM = 128
M_WIDTH = M_HEADS * M_HEAD_DIM
A_HEADS = 8
A_HEAD_DIM = 64
A_WIDTH = A_HEADS * A_HEAD_DIM
D_FF = 4 * D_MODEL
CONV_WIDTH = 5
ROPE_THETA = 10000.0
NORM_EPS = 1e-6
NEG_INF = -1e30
DILATIONS = (1, 4, 16)
HALF_WINDOW = 64

LANES = 128
N_GATES = 4 * M_HEADS
MAIN_COLS = 4 * M_WIDTH + 3 * A_WIDTH
PROJ_COLS = MAIN_COLS + LANES
MV_COL = 2 * M_WIDTH
OGATE_COL = 3 * M_WIDTH
AQ_COL = 4 * M_WIDTH
AK_COL = AQ_COL + A_WIDTH
CHUNK = 128
QB = 128
SPAN = QB + 2 * HALF_WINDOW
VMEM_LIMIT = 56 * 1024 * 1024
LOG2E = 1.4426950408889634
N_GATE_ROWS = 5


def _sigmoid(x):
    return 1.0 / (1.0 + jnp.exp(-x))


def _in_proj_kernel(x_ref, g_ref, wm_ref, wa_ref, wg_ref, rope_ref, o_ref, va_ref, gt_ref):
    x = x_ref[...]
    ms = jnp.mean(x * x, axis=-1, keepdims=True)
    h = (x * lax.rsqrt(ms + NORM_EPS) * g_ref[...]).astype(BF16)
    groups = ((0, wm_ref), (AQ_COL, wa_ref), (MAIN_COLS, wg_ref))
    lane = lax.broadcasted_iota(jnp.int32, (x.shape[0], LANES), 1)
    first_half = (lane % A_HEAD_DIM) < (A_HEAD_DIM // 2)
    for c in range(0, PROJ_COLS, M_WIDTH):
        w = min(M_WIDTH, PROJ_COLS - c)
        base, w_ref = [(b0, ref) for b0, ref in groups if b0 <= c][-1]
        r = jnp.dot(h, w_ref[:, c - base:c - base + w], preferred_element_type=F32)
        if c == OGATE_COL:
            r = _sigmoid(r)
        if c == MV_COL:
            for hd in range(M_HEADS):
                va_ref[:, 2 * hd * LANES:(2 * hd + 1) * LANES] = r[:, hd * LANES:(hd + 1) * LANES].astype(BF16)
                va_ref[:, (2 * hd + 1) * LANES:(2 * hd + 2) * LANES] = jnp.ones((r.shape[0], LANES), BF16)
        if c == MAIN_COLS:
            gt_ref[...] = r.T
        if c in (AQ_COL, AK_COL):
            t = 0 if c == AQ_COL else 2
            cos = rope_ref[:, t * LANES:(t + 1) * LANES]
            sin = rope_ref[:, (t + 1) * LANES:(t + 2) * LANES]
            for j in range(0, w, LANES):
                xj = r[:, j:j + LANES]
                partner = jnp.where(first_half, pltpu.roll(xj, LANES - A_HEAD_DIM // 2, axis=1),
                                    pltpu.roll(xj, A_HEAD_DIM // 2, axis=1))
                o_ref[:, c + j:c + j + LANES] = xj * cos + partner * sin
        else:
            o_ref[:, c:c + w] = r


def _in_proj(x2d, g, weights, rope_tab, layer, seq, tm=512):
    m = x2d.shape[0]
    tiles_per_seq = seq // tm
    resident = lambda a: pl.BlockSpec((None,) + a.shape[1:], lambda i: (layer, 0, 0), pipeline_mode=pl.Buffered(1))
    return pl.pallas_call(
        _in_proj_kernel,
        grid=(m // tm,),
        in_specs=[
            pl.BlockSpec((tm, D_MODEL), lambda i: (i, 0)),
            pl.BlockSpec((None, 1, D_MODEL), lambda i: (layer, 0, 0)),
            *[resident(a) for a in weights],
            pl.BlockSpec((tm, 4 * LANES), lambda i: (i % tiles_per_seq, 0)),
        ],
        out_specs=[
            pl.BlockSpec((tm, PROJ_COLS), lambda i: (i, 0)),
            pl.BlockSpec((tm, 2 * M_WIDTH), lambda i: (i, 0)),
            pl.BlockSpec((LANES, tm), lambda i: (0, i)),
        ],
        out_shape=[
            jax.ShapeDtypeStruct((m, PROJ_COLS), F32),
            jax.ShapeDtypeStruct((m, 2 * M_WIDTH), BF16),
            jax.ShapeDtypeStruct((LANES, m), F32),
        ],
        compiler_params=pltpu.CompilerParams(
            dimension_semantics=("arbitrary",), vmem_limit_bytes=VMEM_LIMIT),
        name="in_proj",
    )(x2d, g, *weights, rope_tab)


def _lane_cumsum(x, reverse):
    lane = lax.broadcasted_iota(jnp.int32, x.shape, 1)
    k = 1
    while k < LANES:
        if reverse:
            x = x + jnp.where(lane < LANES - k, pltpu.roll(x, LANES - k, axis=1), 0.0)
        else:
            x = x + jnp.where(lane >= k, pltpu.roll(x, k, axis=1), 0.0)
        k *= 2
    return x


def _mlstm_kernel(bias_ref, q_ref, k_ref, va_ref, o_ref, li_f_ref, li_b_ref, lf_f_ref, lf_b_ref,
                  cwq_ref, cwk_ref, hg_ref, y_ref,
                  edge_ref, qc_ref, kc_ref, cst_ref, cprev_ref, gl_ref, *, bias_base):
    seq = q_ref.shape[0]
    n_chunks = seq // CHUNK
    head = pl.program_id(1)
    pad = CONV_WIDTH // 2

    U, NLF, MPREV, WROW, DECAY = range(N_GATE_ROWS)
    chunk_id = lax.broadcasted_iota(jnp.int32, (n_chunks, LANES), 0)
    for d, (li_ref, lf_ref) in enumerate(((li_f_ref, lf_f_ref), (li_b_ref, lf_b_ref))):
        li = (li_ref[...] + bias_ref[bias_base + d * M_HEADS + head]) * LOG2E
        fpre = lf_ref[...] + bias_ref[bias_base + 2 * M_HEADS + d * M_HEADS + head]
        lf = (jnp.minimum(fpre, 0.0) - jnp.log(1.0 + jnp.exp(-jnp.abs(fpre)))) * LOG2E
        b = _lane_cumsum(lf, reverse=(d == 1))
        u = li - b
        tot = jnp.broadcast_to(jnp.sum(lf, axis=-1, keepdims=True), lf.shape)
        gmax = jnp.broadcast_to(jnp.max(u, axis=-1, keepdims=True), lf.shape) + tot
        m = jnp.full((1, LANES), NEG_INF, F32)
        m_prev = jnp.zeros((n_chunks, LANES), F32)
        m_new = jnp.zeros((n_chunks, LANES), F32)
        for step in range(n_chunks):
            c = step if d == 0 else n_chunks - 1 - step
            m_prev = jnp.where(chunk_id == c, m, m_prev)
            m = jnp.maximum(tot[c:c + 1, :] + m, gmax[c:c + 1, :])
            m_new = jnp.where(chunk_id == c, m, m_new)
        gl_ref[d, U] = u
        gl_ref[d, NLF] = -lf
        gl_ref[d, MPREV] = m_prev
        gl_ref[d, WROW] = jnp.exp2(u + tot - m_new)
        gl_ref[d, DECAY] = jnp.exp2(tot + m_prev - m_new)

    def conv_chunk(c, src_ref, cw_ref, dst_ref, scale, slot):
        r0 = c * CHUNK
        if c == 0:
            edge_ref[slot, 0:8, :] = jnp.zeros((8, LANES), F32)
            edge_ref[slot, 8:8 + CHUNK + 8, :] = src_ref[0:CHUNK + 8, :]
            window = lambda off: edge_ref[slot, 8 + off:8 + off + CHUNK, :]
        elif c == n_chunks - 1:
            edge_ref[slot + 1, 0:CHUNK + 8, :] = src_ref[seq - CHUNK - 8:seq, :]
            edge_ref[slot + 1, CHUNK + 8:CHUNK + 16, :] = jnp.zeros((8, LANES), F32)
            window = lambda off: edge_ref[slot + 1, 8 + off:8 + off + CHUNK, :]
        else:
            window = lambda off: src_ref[r0 + off:r0 + off + CHUNK, :]
        acc = jnp.zeros((CHUNK, LANES), F32)
        for t in range(CONV_WIDTH):
            acc = acc + cw_ref[t:t + 1, :] * window(t - pad)
        act = acc * _sigmoid(acc)
        if scale != 1.0:
            act = act * scale
        dst_ref[r0:r0 + CHUNK, :] = act.astype(dst_ref.dtype)

    q_conv = lambda c: conv_chunk(c, q_ref, cwq_ref, qc_ref, 1.0, 0)
    k_conv = lambda c: conv_chunk(c, k_ref, cwk_ref, kc_ref, 1.0 / math.sqrt(M_HEAD_DIM), 2)

    cst_ref[...] = jnp.zeros(cst_ref.shape, F32)

    def state_step(i):
        for d in range(2):
            c = i if d == 0 else n_chunks - 1 - i
            rows = pl.ds(c * CHUNK, CHUNK)
            w = gl_ref[d, WROW, c:c + 1, :]
            decay = gl_ref[d, DECAY, c:c + 1, :]
            upd = jnp.dot((kc_ref[rows, :].T * w).astype(BF16), va_ref[rows, :],
                          preferred_element_type=F32)
            state = cst_ref[d]
            cprev_ref[d, c] = state.astype(BF16)
            cst_ref[d] = jnp.concatenate([decay, decay], axis=1) * state + upd

    row = lax.broadcasted_iota(jnp.int32, (CHUNK, CHUNK), 0)
    col = lax.broadcasted_iota(jnp.int32, (CHUNK, CHUNK), 1)
    masks = (col <= row, col >= row)

    def out_chunk(c):
        rows = pl.ds(c * CHUNK, CHUNK)
        q = qc_ref[rows, :]
        va = va_ref[rows, :]
        s = lax.dot_general(q, kc_ref[rows, :].astype(BF16), (((1,), (1,)), ((), ())), preferred_element_type=F32)
        hm = jnp.zeros((CHUNK, LANES), F32)
        for d in range(2):
            mask = masks[d]
            u = gl_ref[d, U, c:c + 1, :]
            nlf = gl_ref[d, NLF, c:c + 1, :]
            m_prev = gl_ref[d, MPREV, c:c + 1, :]
            um = jnp.where(mask, u, NEG_INF)
            mloc = jnp.maximum(jnp.max(um, axis=-1, keepdims=True), m_prev)
            dt = jnp.exp2(um - mloc)
            inter = jnp.exp2(m_prev - mloc)
            lhs = jnp.concatenate([(s * dt).astype(BF16), q * inter.astype(BF16)], axis=1)
            rhs = jnp.concatenate([va, cprev_ref[d, c]], axis=0)
            out = jnp.dot(lhs, rhs, preferred_element_type=F32)
            nb_col = jnp.sum(jnp.where(mask, nlf, 0.0), axis=-1, keepdims=True)
            stab = jnp.exp2(nb_col - mloc)
            hm = hm + out[:, 0:LANES] / jnp.maximum(jnp.abs(out[:, LANES:]), stab)
        ms = jnp.mean(hm * hm, axis=-1, keepdims=True)
        y = hm * lax.rsqrt(ms + NORM_EPS) * hg_ref[...] * o_ref[rows, :]
        y_ref[rows, :] = y.astype(y_ref.dtype)

    for conv in (q_conv, k_conv):
        for c in (0, n_chunks - 1, *range(1, n_chunks - 1)):
            conv(c)
    for i in range(n_chunks):
        state_step(i)
    for c in range(n_chunks):
        out_chunk(c)


def _mlstm(proj3, va3, gates_t, gate_bias, conv_w, head_norm_g, layer):
    batch, seq, _ = proj3.shape
    n_chunks = seq // CHUNK
    col = lambda off: pl.BlockSpec((None, seq, LANES), lambda b, h: (b, 0, off + h))
    gate = lambda off: pl.BlockSpec((None, None, n_chunks, LANES), lambda b, h: (off + h, b, 0, 0))
    return pl.pallas_call(
        functools.partial(_mlstm_kernel, bias_base=layer * N_GATES),
        grid=(batch, M_HEADS),
        in_specs=[
            pl.BlockSpec(memory_space=pltpu.SMEM),
            col(0), col(M_HEADS),
            pl.BlockSpec((None, seq, 2 * LANES), lambda b, h: (b, 0, h)),
            col(OGATE_COL // LANES),
            gate(0), gate(M_HEADS), gate(2 * M_HEADS), gate(3 * M_HEADS),
            pl.BlockSpec((None, CONV_WIDTH, LANES), lambda b, h: (layer, 0, h)),
            pl.BlockSpec((None, CONV_WIDTH, LANES), lambda b, h: (layer, 0, M_HEADS + h)),
            pl.BlockSpec((None, 1, LANES), lambda b, h: (layer, 0, h)),
        ],
        out_specs=pl.BlockSpec((None, seq, LANES), lambda b, h: (b, 0, h)),
        out_shape=jax.ShapeDtypeStruct((batch, seq, M_WIDTH), BF16),
        scratch_shapes=[
            pltpu.VMEM((4, CHUNK + 16, LANES), F32),
            pltpu.VMEM((seq, LANES), BF16),
            pltpu.VMEM((seq, LANES), F32),
            pltpu.VMEM((2, M_HEAD_DIM, 2 * LANES), F32),
            pltpu.VMEM((2, n_chunks, M_HEAD_DIM, 2 * LANES), BF16),
            pltpu.VMEM((2, N_GATE_ROWS, n_chunks, LANES), F32),
        ],
        compiler_params=pltpu.CompilerParams(
            dimension_semantics=("arbitrary", "arbitrary"), vmem_limit_bytes=VMEM_LIMIT),
        name="mlstm",
    )(gate_bias, proj3, proj3, va3, proj3, gates_t, gates_t, gates_t, gates_t,
      conv_w, conv_w, head_norm_g)


def _attn_kernel(q_ref, k_ref, v_ref, bias_ref, bias8_ref, y_ref,
                 c4_ref, c8_ref, num_ref, den_ref, mx_ref):
    seq = q_ref.shape[0]
    rope_rows = 256

    d4, d16 = DILATIONS[1], DILATIONS[2]
    d8 = d16 // 2
    len4, len8 = seq // d4, seq // d8

    def split4(r, l0):
        for t, src in enumerate((q_ref, k_ref, v_ref)):
            c4_ref[t, r * len4 + l0:r * len4 + l0 + rope_rows, :] = src[pl.ds(r + d4 * l0, rope_rows, stride=d4), :]

    def split8(r):
        for t in range(3):
            c8_ref[t, r * len8:(r + 1) * len8, :] = c4_ref[t, pl.ds((r % d4) * len4 + r // d4, len8, stride=d8 // d4), :]

    lane_q = lax.broadcasted_iota(jnp.int32, (QB, LANES), 1)
    head0_q = lane_q < A_HEAD_DIM

    def block(q, k, v, bias):
        span = k.shape[0]
        q2 = jnp.concatenate([jnp.where(head0_q, q, 0.0), jnp.where(head0_q, 0.0, q)], axis=0).astype(BF16)
        s = lax.dot_general(q2, k.astype(BF16), (((1,), (1,)), ((), ())), preferred_element_type=F32)
        s = s + jnp.concatenate([bias, bias], axis=0)
        m = jnp.max(s, axis=-1, keepdims=True)
        p = jnp.exp2(s - m).astype(BF16)
        rhs = jnp.concatenate([v.astype(BF16), jnp.ones((span, LANES), BF16)], axis=1)
        acc = jnp.dot(p, rhs, preferred_element_type=F32)
        return (jnp.where(head0_q, acc[0:QB, 0:LANES], acc[QB:, 0:LANES]),
                jnp.where(head0_q, acc[0:QB, LANES:], acc[QB:, LANES:]),
                jnp.where(head0_q, m[0:QB], m[QB:2 * QB]))

    stats_refs = (num_ref, den_ref, mx_ref)

    nat = (lambda rows: q_ref[rows, :], lambda rows: k_ref[rows, :], lambda rows: v_ref[rows, :])
    cm4 = tuple((lambda rows, t=t: c4_ref[t, rows, :]) for t in range(3))
    cm8 = tuple((lambda rows, t=t: c8_ref[t, rows, :]) for t in range(3))
    patterns = ((DILATIONS[0], HALF_WINDOW, nat, bias_ref),
                (d4, HALF_WINDOW, cm4, bias_ref),
                (d8, HALF_WINDOW * (d16 // d8), cm8, bias8_ref))

    def band_block(pat, r, j):
        mod, half, srcs, biases = patterns[pat]
        length = seq // mod
        q0 = j * QB
        k0 = min(max(q0 - half, 0), length - SPAN)
        qrows = pl.ds(r * length + q0, QB)
        krows = pl.ds(r * length + k0, SPAN)
        out_rows = qrows if mod == 1 else pl.ds(r + mod * q0, QB, stride=mod)
        res = block(srcs[0](qrows), srcs[1](krows), srcs[2](krows), biases[(q0 - k0) // half])
        for ref, val in zip(stats_refs, res):
            ref[pat, out_rows, :] = val

    def merge(c):
        rows = pl.ds(c * rope_rows, rope_rows)
        m_all = jnp.maximum(jnp.maximum(mx_ref[0, rows, :], mx_ref[1, rows, :]), mx_ref[2, rows, :])
        num = jnp.zeros((rope_rows, LANES), F32)
        den = jnp.zeros((rope_rows, LANES), F32)
        for pat in range(len(DILATIONS)):
            w = jnp.exp2(mx_ref[pat, rows, :] - m_all)
            num = num + w * num_ref[pat, rows, :]
            den = den + w * den_ref[pat, rows, :]
        y_ref[rows, :] = (num / den).astype(y_ref.dtype)

    for r in range(d4):
        for l0 in range(0, len4, rope_rows):
            split4(r, l0)
    for r in range(d8):
        split8(r)
    n_groups = len8 // QB
    group_rows = seq // n_groups
    for g in range(n_groups):
        for pat in reversed(range(len(patterns))):
            mod = patterns[pat][0]
            per_group = (seq // mod // QB) // n_groups
            for r in range(mod):
                for j in range(g * per_group, (g + 1) * per_group):
                    band_block(pat, r, j)
        for c in range(g * group_rows // rope_rows, (g + 1) * group_rows // rope_rows):
            merge(c)


def _attention(proj3, bias, bias8):
    batch, seq, _ = proj3.shape
    assert DILATIONS == (1, 4, 16) and seq * 2 // DILATIONS[2] == SPAN
    n_pairs = A_WIDTH // LANES
    base = 4 * M_WIDTH // LANES
    col = lambda off: pl.BlockSpec((None, seq, LANES), lambda b, p: (b, 0, base + off + p))
    const2 = lambda shape: pl.BlockSpec(shape, lambda b, p: (0,) * len(shape))
    n_pat = len(DILATIONS)
    return pl.pallas_call(
        _attn_kernel,
        grid=(batch, n_pairs),
        in_specs=[
            col(0), col(n_pairs), col(2 * n_pairs),
            const2((3, QB, SPAN)), const2((2, QB, SPAN)),
        ],
        out_specs=pl.BlockSpec((None, seq, LANES), lambda b, p: (b, 0, p)),
        out_shape=jax.ShapeDtypeStruct((batch, seq, A_WIDTH), BF16),
        scratch_shapes=[
            pltpu.VMEM((3, seq, LANES), F32),
            pltpu.VMEM((3, seq, LANES), F32),
            pltpu.VMEM((n_pat, seq, LANES), F32),
            pltpu.VMEM((n_pat, seq, LANES), F32),
            pltpu.VMEM((n_pat, seq, LANES), F32),
        ],
        compiler_params=pltpu.CompilerParams(
            dimension_semantics=("arbitrary", "arbitrary"), vmem_limit_bytes=VMEM_LIMIT),
        name="dilated_attn",
    )(proj3, proj3, proj3, bias, bias8)


def _out_mlp_kernel(x_ref, ym_ref, ya_ref, wo_ref, g2_ref, wu_ref, wd_ref, fg_ref, o_ref, usq_ref,
                    *, ff_chunk, final):
    x1 = (x_ref[...]
          + jnp.dot(ym_ref[...], wo_ref[0:M_WIDTH, :], preferred_element_type=F32)
          + jnp.dot(ya_ref[...], wo_ref[M_WIDTH:, :], preferred_element_type=F32))
    ms = jnp.mean(x1 * x1, axis=-1, keepdims=True)
    h = (x1 * lax.rsqrt(ms + NORM_EPS) * g2_ref[...]).astype(BF16)
    for c in range(0, D_FF, ff_chunk):
        u = jnp.maximum(jnp.dot(h, wu_ref[:, c:c + ff_chunk], preferred_element_type=F32), 0.0)
        usq_ref[:, c:c + ff_chunk] = (u * u).astype(BF16)
    out = x1 + jnp.dot(usq_ref[...], wd_ref[...], preferred_element_type=F32)
    if final:
        ms = jnp.mean(out * out, axis=-1, keepdims=True)
        out = out * lax.rsqrt(ms + NORM_EPS) * fg_ref[...]
    o_ref[...] = out


def _out_mlp(x2d, ym, ya, w_out, g2, w_up, w_down, final_g, layer, final, tm=512):
    m = x2d.shape[0]
    resident = lambda shape: pl.BlockSpec((None,) + shape, lambda i: (layer, 0, 0), pipeline_mode=pl.Buffered(1))
    return pl.pallas_call(
        functools.partial(_out_mlp_kernel, ff_chunk=1024, final=final),
        grid=(m // tm,),
        in_specs=[
            pl.BlockSpec((tm, D_MODEL), lambda i: (i, 0)),
            pl.BlockSpec((tm, M_WIDTH), lambda i: (i, 0)),
            pl.BlockSpec((tm, A_WIDTH), lambda i: (i, 0)),
            resident((D_MODEL, D_MODEL)),
            resident((1, D_MODEL)),
            resident((D_MODEL, D_FF)),
            resident((D_FF, D_MODEL)),
            pl.BlockSpec((1, D_MODEL), lambda i: (0, 0), pipeline_mode=pl.Buffered(1)),
        ],
        out_specs=pl.BlockSpec((tm, D_MODEL), lambda i: (i, 0)),
        out_shape=jax.ShapeDtypeStruct((m, D_MODEL), F32),
        scratch_shapes=[pltpu.VMEM((tm, D_FF), BF16)],
        compiler_params=pltpu.CompilerParams(
            dimension_semantics=("arbitrary",), vmem_limit_bytes=VMEM_LIMIT),
        name="out_mlp",
    )(x2d, ym, ya, w_out, g2, w_up, w_down, final_g)


def _rope_tables(seq):
    half = A_HEAD_DIM // 2
    f32 = np.float32
    inv_freq = f32(ROPE_THETA) ** (-np.arange(half, dtype=f32) / f32(half))
    ang = np.arange(seq, dtype=f32)[:, None] * inv_freq[None, :]
    cos = np.tile(np.cos(ang), (1, LANES // half)).astype(f32)
    sin = np.sin(ang).astype(f32)
    sin = np.concatenate([-sin, sin, -sin, sin], axis=1)
    q_scale = f32(LOG2E / math.sqrt(A_HEAD_DIM))
    return jnp.asarray(np.concatenate([cos * q_scale, sin * q_scale, cos, sin], axis=1))


def _band_bias():
    row = np.arange(QB)[:, None]
    col = np.arange(SPAN)[None, :]
    variants = []
    for off in (0, HALF_WINDOW, 2 * HALF_WINDOW):
        rel = col - off - row
        variants.append(np.where(np.abs(rel) <= HALF_WINDOW, 0.0, NEG_INF))
    variants8 = []
    for off in (0, QB):
        rel = col - off - row
        variants8.append(np.where((np.abs(rel) <= 2 * HALF_WINDOW) & (rel % 2 == 0), 0.0, NEG_INF))
    return (jnp.asarray(np.stack(variants).astype(np.float32)),
            jnp.asarray(np.stack(variants8).astype(np.float32)))


def kernel(x, norm1_g, w_in, conv_w, gate_i_b, gate_f_b, head_norm_g, w_out, norm2_g, w_up, w_down, final_g):
    batch, seq, _ = x.shape
    depth = w_in.shape[0]
    n_chunks = seq // CHUNK
    rope_tab = _rope_tables(seq)
    bias, bias8 = _band_bias()
    gate_lo = 4 * M_WIDTH
    attn_lo = gate_lo + N_GATES

    w_gates = jnp.pad(w_in[:, :, gate_lo:attn_lo], ((0, 0), (0, 0), (0, LANES - N_GATES)))
    w_groups = tuple(a.astype(BF16) for a in (w_in[:, :, :gate_lo], w_in[:, :, attn_lo:], w_gates))
    w_out_b, w_up_b, w_down_b = w_out.astype(BF16), w_up.astype(BF16), w_down.astype(BF16)
    gate_bias = jnp.concatenate([gate_i_b, gate_f_b], axis=1).reshape(depth * N_GATES).astype(F32)
    norm1_3, norm2_3, head_g3 = norm1_g[:, None, :], norm2_g[:, None, :], head_norm_g[:, None, :]

    x2d = x.reshape(batch * seq, D_MODEL)
    for layer in range(depth):
        proj, va, gates_t = _in_proj(x2d, norm1_3, w_groups, rope_tab, layer, seq)
        proj3 = proj.reshape(batch, seq, PROJ_COLS)
        ym = _mlstm(proj3, va.reshape(batch, seq, 2 * M_WIDTH), gates_t.reshape(LANES, batch, n_chunks, CHUNK),
                    gate_bias, conv_w, head_g3, layer)
        ya = _attention(proj3, bias, bias8)
        x2d = _out_mlp(x2d, ym.reshape(batch * seq, M_WIDTH), ya.reshape(batch * seq, A_WIDTH),
                       w_out_b, norm2_3, w_up_b, w_down_b, final_g[None, :],
                       layer, final=(layer == depth - 1))
    return x2d.reshape(batch, seq, D_MODEL)
```

```python
import functools
import math

import jax
import jax.numpy as jnp
import numpy as np
from jax import lax
from jax.experimental import pallas as pl
from jax.experimental.pallas import tpu as pltpu

F32 = jnp.float32
BF16 = jnp.bfloat16

D_MODEL = 1024
M_HEADS = 4
M_HEAD_DIM = 128
M_WIDTH = M_HEADS * M_HEAD_DIM
A_HEADS = 8
A_HEAD_DIM = 64
A_WIDTH = A_HEADS * A_HEAD_DIM
D_FF = 4 * D_MODEL
CONV_WIDTH = 5
ROPE_THETA = 10000.0
NORM_EPS = 1e-6
NEG_INF = -1e30
DILATIONS = (1, 4, 16)
HALF_WINDOW = 64

LANES = 128
N_GATES = 4 * M_HEADS
MAIN_COLS = 4 * M_WIDTH + 3 * A_WIDTH
PROJ_COLS = MAIN_COLS + LANES
MV_COL = 2 * M_WIDTH
OGATE_COL = 3 * M_WIDTH
AQ_COL = 4 * M_WIDTH
AK_COL = AQ_COL + A_WIDTH
CHUNK = 128
QB = 128
SPAN = QB + 2 * HALF_WINDOW
VMEM_LIMIT = 56 * 1024 * 1024
LOG2E = 1.4426950408889634
N_GATE_ROWS = 5


def _sigmoid(x):
    return 1.0 / (1.0 + jnp.exp(-x))


def _split_w_in_kernel(w_ref, wm_ref, wa_ref, wg_ref):
    gate_lo = 4 * M_WIDTH
    rows = w_ref.shape[0]
    wm_ref[...] = w_ref[:, 0:gate_lo].astype(BF16)
    tail = w_ref[:, gate_lo:]
    wa_ref[...] = tail[:, N_GATES:].astype(BF16)
    wg_ref[...] = jnp.concatenate([tail[:, 0:N_GATES], jnp.zeros((rows, LANES - N_GATES), F32)], axis=1).astype(BF16)


def _split_w_in(w_in, tr=256):
    depth, rows, cols = w_in.shape
    gate_lo = 4 * M_WIDTH
    widths = (gate_lo, cols - gate_lo - N_GATES, LANES)
    return pl.pallas_call(
        _split_w_in_kernel,
        grid=(depth, rows // tr),
        in_specs=[pl.BlockSpec((None, tr, cols), lambda l, i: (l, i, 0))],
        out_specs=[pl.BlockSpec((None, tr, w), lambda l, i: (l, i, 0)) for w in widths],
        out_shape=[jax.ShapeDtypeStruct((depth, rows, w), BF16) for w in widths],
        compiler_params=pltpu.CompilerParams(
            dimension_semantics=("arbitrary", "arbitrary"), vmem_limit_bytes=VMEM_LIMIT),
        name="split_w_in",
    )(w_in)


def _in_proj_kernel(x_ref, g_ref, wm_ref, wa_ref, wg_ref, rope_ref, o_ref, va_ref, gt_ref):
    x = x_ref[...]
    ms = jnp.mean(x * x, axis=-1, keepdims=True)
    h = (x * lax.rsqrt(ms + NORM_EPS) * g_ref[...]).astype(BF16)
    groups = ((0, wm_ref), (AQ_COL, wa_ref), (MAIN_COLS, wg_ref))
    lane = lax.broadcasted_iota(jnp.int32, (x.shape[0], LANES), 1)
    first_half = (lane % A_HEAD_DIM) < (A_HEAD_DIM // 2)
    for c in range(0, PROJ_COLS, M_WIDTH):
        w = min(M_WIDTH, PROJ_COLS - c)
        base, w_ref = [(b0, ref) for b0, ref in groups if b0 <= c][-1]
        r = jnp.dot(h, w_ref[:, c - base:c - base + w], preferred_element_type=F32)
        if c == OGATE_COL:
            r = _sigmoid(r)
        if c == MV_COL:
            for hd in range(M_HEADS):
                va_ref[:, 2 * hd * LANES:(2 * hd + 1) * LANES] = r[:, hd * LANES:(hd + 1) * LANES].astype(BF16)
                va_ref[:, (2 * hd + 1) * LANES:(2 * hd + 2) * LANES] = jnp.ones((r.shape[0], LANES), BF16)
        if c == MAIN_COLS:
            gt_ref[...] = r.T
        if c in (AQ_COL, AK_COL):
            t = 0 if c == AQ_COL else 2
            cos = rope_ref[:, t * LANES:(t + 1) * LANES]
            sin = rope_ref[:, (t + 1) * LANES:(t + 2) * LANES]
            for j in range(0, w, LANES):
                xj = r[:, j:j + LANES]
                partner = jnp.where(first_half, pltpu.roll(xj, LANES - A_HEAD_DIM // 2, axis=1),
                                    pltpu.roll(xj, A_HEAD_DIM // 2, axis=1))
                o_ref[:, c + j:c + j + LANES] = xj * cos + partner * sin
        else:
            o_ref[:, c:c + w] = r


def _in_proj(x2d, g, weights, rope_tab, layer, seq, tm=512):
    m = x2d.shape[0]
    tiles_per_seq = seq // tm
    resident = lambda a: pl.BlockSpec((None,) + a.shape[1:], lambda i: (layer, 0, 0), pipeline_mode=pl.Buffered(1))
    return pl.pallas_call(
        _in_proj_kernel,
        grid=(m // tm,),
        in_specs=[
            pl.BlockSpec((tm, D_MODEL), lambda i: (i, 0)),
            pl.BlockSpec((None, 1, D_MODEL), lambda i: (layer, 0, 0)),
            *[resident(a) for a in weights],
            pl.BlockSpec((tm, 4 * LANES), lambda i: (i % tiles_per_seq, 0)),
        ],
        out_specs=[
            pl.BlockSpec((tm, PROJ_COLS), lambda i: (i, 0)),
            pl.BlockSpec((tm, 2 * M_WIDTH), lambda i: (i, 0)),
            pl.BlockSpec((LANES, tm), lambda i: (0, i)),
        ],
        out_shape=[
            jax.ShapeDtypeStruct((m, PROJ_COLS), F32),
            jax.ShapeDtypeStruct((m, 2 * M_WIDTH), BF16),
            jax.ShapeDtypeStruct((LANES, m), F32),
        ],
        compiler_params=pltpu.CompilerParams(
            dimension_semantics=("arbitrary",), vmem_limit_bytes=VMEM_LIMIT),
        name="in_proj",
    )(x2d, g, *weights, rope_tab)


def _lane_cumsum(x, reverse):
    lane = lax.broadcasted_iota(jnp.int32, x.shape, 1)
    k = 1
    while k < LANES:
        if reverse:
            x = x + jnp.where(lane < LANES - k, pltpu.roll(x, LANES - k, axis=1), 0.0)
        else:
            x = x + jnp.where(lane >= k, pltpu.roll(x, k, axis=1), 0.0)
        k *= 2
    return x


def _mlstm_kernel(bias_ref, q_ref, k_ref, va_ref, o_ref, li_f_ref, li_b_ref, lf_f_ref, lf_b_ref,
                  cwq_ref, cwk_ref, hg_ref, y_ref,
                  edge_ref, qc_ref, kc_ref, cst_ref, cprev_ref, gl_ref, *, bias_base):
    seq = q_ref.shape[0]
    n_chunks = seq // CHUNK
    head = pl.program_id(1)
    pad = CONV_WIDTH // 2

    U, NLF, MPREV, WROW, DECAY = range(N_GATE_ROWS)
    chunk_id = lax.broadcasted_iota(jnp.int32, (n_chunks, LANES), 0)
    for d, (li_ref, lf_ref) in enumerate(((li_f_ref, lf_f_ref), (li_b_ref, lf_b_ref))):
        li = (li_ref[...] + bias_ref[bias_base + d * M_HEADS + head]) * LOG2E
        fpre = lf_ref[...] + bias_ref[bias_base + 2 * M_HEADS + d * M_HEADS + head]
        lf = (jnp.minimum(fpre, 0.0) - jnp.log(1.0 + jnp.exp(-jnp.abs(fpre)))) * LOG2E
        b = _lane_cumsum(lf, reverse=(d == 1))
        u = li - b
        tot = jnp.broadcast_to(jnp.sum(lf, axis=-1, keepdims=True), lf.shape)
        gmax = jnp.broadcast_to(jnp.max(u, axis=-1, keepdims=True), lf.shape) + tot
        m = jnp.full((1, LANES), NEG_INF, F32)
        m_prev = jnp.zeros((n_chunks, LANES), F32)
        m_new = jnp.zeros((n_chunks, LANES), F32)
        for step in range(n_chunks):
            c = step if d == 0 else n_chunks - 1 - step
            m_prev = jnp.where(chunk_id == c, m, m_prev)
            m = jnp.maximum(tot[c:c + 1, :] + m, gmax[c:c + 1, :])
            m_new = jnp.where(chunk_id == c, m, m_new)
        gl_ref[d, U] = u
        gl_ref[d, NLF] = -lf
        gl_ref[d, MPREV] = m_prev
        gl_ref[d, WROW] = jnp.exp2(u + tot - m_new)
        gl_ref[d, DECAY] = jnp.exp2(tot + m_prev - m_new)

    def conv_chunk(c, src_ref, cw_ref, dst_ref, scale, slot):
        r0 = c * CHUNK
        if c == 0:
            edge_ref[slot, 0:8, :] = jnp.zeros((8, LANES), F32)
            edge_ref[slot, 8:8 + CHUNK + 8, :] = src_ref[0:CHUNK + 8, :]
            window = lambda off: edge_ref[slot, 8 + off:8 + off + CHUNK, :]
        elif c == n_chunks - 1:
            edge_ref[slot + 1, 0:CHUNK + 8, :] = src_ref[seq - CHUNK - 8:seq, :]
            edge_ref[slot + 1, CHUNK + 8:CHUNK + 16, :] = jnp.zeros((8, LANES), F32)
            window = lambda off: edge_ref[slot + 1, 8 + off:8 + off + CHUNK, :]
        else:
            window = lambda off: src_ref[r0 + off:r0 + off + CHUNK, :]
        acc = jnp.zeros((CHUNK, LANES), F32)
        for t in range(CONV_WIDTH):
            acc = acc + cw_ref[t:t + 1, :] * window(t - pad)
        act = acc * _sigmoid(acc)
        if scale != 1.0:
            act = act * scale
        dst_ref[r0:r0 + CHUNK, :] = act.astype(dst_ref.dtype)

    q_conv = lambda c: conv_chunk(c, q_ref, cwq_ref, qc_ref, 1.0, 0)
    k_conv = lambda c: conv_chunk(c, k_ref, cwk_ref, kc_ref, 1.0 / math.sqrt(M_HEAD_DIM), 2)

    cst_ref[...] = jnp.zeros(cst_ref.shape, F32)

    def state_step(i):
        for d in range(2):
            c = i if d == 0 else n_chunks - 1 - i
            rows = pl.ds(c * CHUNK, CHUNK)
            w = gl_ref[d, WROW, c:c + 1, :]
            decay = gl_ref[d, DECAY, c:c + 1, :]
            upd = jnp.dot((kc_ref[rows, :].T * w).astype(BF16), va_ref[rows, :],
                          preferred_element_type=F32)
            state = cst_ref[d]
            cprev_ref[d, c] = state.astype(BF16)
            cst_ref[d] = jnp.concatenate([decay, decay], axis=1) * state + upd

    row = lax.broadcasted_iota(jnp.int32, (CHUNK, CHUNK), 0)
    col = lax.broadcasted_iota(jnp.int32, (CHUNK, CHUNK), 1)
    masks = (col <= row, col >= row)

    def out_chunk(c):
        rows = pl.ds(c * CHUNK, CHUNK)
        q = qc_ref[rows, :]
        va = va_ref[rows, :]
        s = lax.dot_general(q, kc_ref[rows, :].astype(BF16), (((1,), (1,)), ((), ())), preferred_element_type=F32)
        hm = jnp.zeros((CHUNK, LANES), F32)
        for d in range(2):
            mask = masks[d]
            u = gl_ref[d, U, c:c + 1, :]
            nlf = gl_ref[d, NLF, c:c + 1, :]
            m_prev = gl_ref[d, MPREV, c:c + 1, :]
            um = jnp.where(mask, u, NEG_INF)
            mloc = jnp.maximum(jnp.max(um, axis=-1, keepdims=True), m_prev)
            dt = jnp.exp2(um - mloc)
            inter = jnp.exp2(m_prev - mloc)
            lhs = jnp.concatenate([(s * dt).astype(BF16), q * inter.astype(BF16)], axis=1)
            rhs = jnp.concatenate([va, cprev_ref[d, c]], axis=0)
            out = jnp.dot(lhs, rhs, preferred_element_type=F32)
            nb_col = jnp.sum(jnp.where(mask, nlf, 0.0), axis=-1, keepdims=True)
            stab = jnp.exp2(nb_col - mloc)
            hm = hm + out[:, 0:LANES] / jnp.maximum(jnp.abs(out[:, LANES:]), stab)
        ms = jnp.mean(hm * hm, axis=-1, keepdims=True)
        y = hm * lax.rsqrt(ms + NORM_EPS) * hg_ref[...] * o_ref[rows, :]
        y_ref[rows, :] = y.astype(y_ref.dtype)

    for conv in (q_conv, k_conv):
        for c in (0, n_chunks - 1, *range(1, n_chunks - 1)):
            conv(c)
    for i in range(n_chunks):
        state_step(i)
    for c in range(n_chunks):
        out_chunk(c)


def _mlstm(proj3, va3, gates_t, gate_bias, conv_w, head_norm_g, layer):
    batch, seq, _ = proj3.shape
    n_chunks = seq // CHUNK
    col = lambda off: pl.BlockSpec((None, seq, LANES), lambda b, h: (b, 0, off + h))
    gate = lambda off: pl.BlockSpec((None, None, n_chunks, LANES), lambda b, h: (off + h, b, 0, 0))
    return pl.pallas_call(
        functools.partial(_mlstm_kernel, bias_base=layer * N_GATES),
        grid=(batch, M_HEADS),
        in_specs=[
            pl.BlockSpec(memory_space=pltpu.SMEM),
            col(0), col(M_HEADS),
            pl.BlockSpec((None, seq, 2 * LANES), lambda b, h: (b, 0, h)),
            col(OGATE_COL // LANES),
            gate(0), gate(M_HEADS), gate(2 * M_HEADS), gate(3 * M_HEADS),
            pl.BlockSpec((None, CONV_WIDTH, LANES), lambda b, h: (layer, 0, h)),
            pl.BlockSpec((None, CONV_WIDTH, LANES), lambda b, h: (layer, 0, M_HEADS + h)),
            pl.BlockSpec((None, 1, LANES), lambda b, h: (layer, 0, h)),
        ],
        out_specs=pl.BlockSpec((None, seq, LANES), lambda b, h: (b, 0, h)),
        out_shape=jax.ShapeDtypeStruct((batch, seq, M_WIDTH), BF16),
        scratch_shapes=[
            pltpu.VMEM((4, CHUNK + 16, LANES), F32),
            pltpu.VMEM((seq, LANES), BF16),
            pltpu.VMEM((seq, LANES), F32),
            pltpu.VMEM((2, M_HEAD_DIM, 2 * LANES), F32),
            pltpu.VMEM((2, n_chunks, M_HEAD_DIM, 2 * LANES), BF16),
            pltpu.VMEM((2, N_GATE_ROWS, n_chunks, LANES), F32),
        ],
        compiler_params=pltpu.CompilerParams(
            dimension_semantics=("arbitrary", "arbitrary"), vmem_limit_bytes=VMEM_LIMIT),
        name="mlstm",
    )(gate_bias, proj3, proj3, va3, proj3, gates_t, gates_t, gates_t, gates_t,
      conv_w, conv_w, head_norm_g)


def _attn_kernel(q_ref, k_ref, v_ref, bias_ref, bias8_ref, y_ref,
                 c4_ref, c8_ref, num_ref, den_ref, mx_ref):
    seq = q_ref.shape[0]
    rope_rows = 256

    d4, d16 = DILATIONS[1], DILATIONS[2]
    d8 = d16 // 2
    len4, len8 = seq // d4, seq // d8

    def split4(r, l0):
        for t, src in enumerate((q_ref, k_ref, v_ref)):
            c4_ref[t, r * len4 + l0:r * len4 + l0 + rope_rows, :] = src[pl.ds(r + d4 * l0, rope_rows, stride=d4), :]

    def split8(r):
        for t in range(3):
            c8_ref[t, r * len8:(r + 1) * len8, :] = c4_ref[t, pl.ds((r % d4) * len4 + r // d4, len8, stride=d8 // d4), :]

    lane_q = lax.broadcasted_iota(jnp.int32, (QB, LANES), 1)
    head0_q = lane_q < A_HEAD_DIM

    def block(q, k, v, bias):
        span = k.shape[0]
        q2 = jnp.concatenate([jnp.where(head0_q, q, 0.0), jnp.where(head0_q, 0.0, q)], axis=0).astype(BF16)
        s = lax.dot_general(q2, k.astype(BF16), (((1,), (1,)), ((), ())), preferred_element_type=F32)
        s = s + jnp.concatenate([bias, bias], axis=0)
        m = jnp.max(s, axis=-1, keepdims=True)
        p = jnp.exp2(s - m).astype(BF16)
        rhs = jnp.concatenate([v.astype(BF16), jnp.ones((span, LANES), BF16)], axis=1)
        acc = jnp.dot(p, rhs, preferred_element_type=F32)
        return (jnp.where(head0_q, acc[0:QB, 0:LANES], acc[QB:, 0:LANES]),
                jnp.where(head0_q, acc[0:QB, LANES:], acc[QB:, LANES:]),
                jnp.where(head0_q, m[0:QB], m[QB:2 * QB]))

    stats_refs = (num_ref, den_ref, mx_ref)

    nat = (lambda rows: q_ref[rows, :], lambda rows: k_ref[rows, :], lambda rows: v_ref[rows, :])
    cm4 = tuple((lambda rows, t=t: c4_ref[t, rows, :]) for t in range(3))
    cm8 = tuple((lambda rows, t=t: c8_ref[t, rows, :]) for t in range(3))
    patterns = ((DILATIONS[0], HALF_WINDOW, nat, bias_ref),
                (d4, HALF_WINDOW, cm4, bias_ref),
                (d8, HALF_WINDOW * (d16 // d8), cm8, bias8_ref))

    def band_block(pat, r, j):
        mod, half, srcs, biases = patterns[pat]
        length = seq // mod
        q0 = j * QB
        k0 = min(max(q0 - half, 0), length - SPAN)
        qrows = pl.ds(r * length + q0, QB)
        krows = pl.ds(r * length + k0, SPAN)
        out_rows = qrows if mod == 1 else pl.ds(r + mod * q0, QB, stride=mod)
        res = block(srcs[0](qrows), srcs[1](krows), srcs[2](krows), biases[(q0 - k0) // half])
        for ref, val in zip(stats_refs, res):
            ref[pat, out_rows, :] = val

    def merge(c):
        rows = pl.ds(c * rope_rows, rope_rows)
        m_all = jnp.maximum(jnp.maximum(mx_ref[0, rows, :], mx_ref[1, rows, :]), mx_ref[2, rows, :])
        num = jnp.zeros((rope_rows, LANES), F32)
        den = jnp.zeros((rope_rows, LANES), F32)
        for pat in range(len(DILATIONS)):
            w = jnp.exp2(mx_ref[pat, rows, :] - m_all)
            num = num + w * num_ref[pat, rows, :]
            den = den + w * den_ref[pat, rows, :]
        y_ref[rows, :] = (num / den).astype(y_ref.dtype)

    for r in range(d4):
        for l0 in range(0, len4, rope_rows):
            split4(r, l0)
    for r in range(d8):
        split8(r)
    n_groups = len8 // QB
    group_rows = seq // n_groups
    for g in range(n_groups):
        for pat in reversed(range(len(patterns))):
            mod = patterns[pat][0]
            per_group = (seq // mod // QB) // n_groups
            for r in range(mod):
                for j in range(g * per_group, (g + 1) * per_group):
                    band_block(pat, r, j)
        for c in range(g * group_rows // rope_rows, (g + 1) * group_rows // rope_rows):
            merge(c)


def _attention(proj3, bias, bias8):
    batch, seq, _ = proj3.shape
    assert DILATIONS == (1, 4, 16) and seq * 2 // DILATIONS[2] == SPAN
    n_pairs = A_WIDTH // LANES
    base = 4 * M_WIDTH // LANES
    col = lambda off: pl.BlockSpec((None, seq, LANES), lambda b, p: (b, 0, base + off + p))
    const2 = lambda shape: pl.BlockSpec(shape, lambda b, p: (0,) * len(shape))
    n_pat = len(DILATIONS)
    return pl.pallas_call(
        _attn_kernel,
        grid=(batch, n_pairs),
        in_specs=[
            col(0), col(n_pairs), col(2 * n_pairs),
            const2((3, QB, SPAN)), const2((2, QB, SPAN)),
        ],
        out_specs=pl.BlockSpec((None, seq, LANES), lambda b, p: (b, 0, p)),
        out_shape=jax.ShapeDtypeStruct((batch, seq, A_WIDTH), BF16),
        scratch_shapes=[
            pltpu.VMEM((3, seq, LANES), F32),
            pltpu.VMEM((3, seq, LANES), F32),
            pltpu.VMEM((n_pat, seq, LANES), F32),
            pltpu.VMEM((n_pat, seq, LANES), F32),
            pltpu.VMEM((n_pat, seq, LANES), F32),
        ],
        compiler_params=pltpu.CompilerParams(
            dimension_semantics=("arbitrary", "arbitrary"), vmem_limit_bytes=VMEM_LIMIT),
        name="dilated_attn",
    )(proj3, proj3, proj3, bias, bias8)


def _out_mlp_kernel(x_ref, ym_ref, ya_ref, wo_ref, g2_ref, wu_ref, wd_ref, fg_ref, o_ref, usq_ref,
                    *, ff_chunk, final):
    x1 = (x_ref[...]
          + jnp.dot(ym_ref[...], wo_ref[0:M_WIDTH, :], preferred_element_type=F32)
          + jnp.dot(ya_ref[...], wo_ref[M_WIDTH:, :], preferred_element_type=F32))
    ms = jnp.mean(x1 * x1, axis=-1, keepdims=True)
    h = (x1 * lax.rsqrt(ms + NORM_EPS) * g2_ref[...]).astype(BF16)
    for c in range(0, D_FF, ff_chunk):
        u = jnp.maximum(jnp.dot(h, wu_ref[:, c:c + ff_chunk], preferred_element_type=F32), 0.0)
        usq_ref[:, c:c + ff_chunk] = (u * u).astype(BF16)
    out = x1 + jnp.dot(usq_ref[...], wd_ref[...], preferred_element_type=F32)
    if final:
        ms = jnp.mean(out * out, axis=-1, keepdims=True)
        out = out * lax.rsqrt(ms + NORM_EPS) * fg_ref[...]
    o_ref[...] = out


def _out_mlp(x2d, ym, ya, w_out, g2, w_up, w_down, final_g, layer, final, tm=512):
    m = x2d.shape[0]
    resident = lambda shape: pl.BlockSpec((None,) + shape, lambda i: (layer, 0, 0), pipeline_mode=pl.Buffered(1))
    return pl.pallas_call(
        functools.partial(_out_mlp_kernel, ff_chunk=1024, final=final),
        grid=(m // tm,),
        in_specs=[
            pl.BlockSpec((tm, D_MODEL), lambda i: (i, 0)),
            pl.BlockSpec((tm, M_WIDTH), lambda i: (i, 0)),
            pl.BlockSpec((tm, A_WIDTH), lambda i: (i, 0)),
            resident((D_MODEL, D_MODEL)),
            resident((1, D_MODEL)),
            resident((D_MODEL, D_FF)),
            resident((D_FF, D_MODEL)),
            pl.BlockSpec((1, D_MODEL), lambda i: (0, 0), pipeline_mode=pl.Buffered(1)),
        ],
        out_specs=pl.BlockSpec((tm, D_MODEL), lambda i: (i, 0)),
        out_shape=jax.ShapeDtypeStruct((m, D_MODEL), F32),
        scratch_shapes=[pltpu.VMEM((tm, D_FF), BF16)],
        compiler_params=pltpu.CompilerParams(
            dimension_semantics=("arbitrary",), vmem_limit_bytes=VMEM_LIMIT),
        name="out_mlp",
    )(x2d, ym, ya, w_out, g2, w_up, w_down, final_g)


def _rope_tables(seq):
    half = A_HEAD_DIM // 2
    f32 = np.float32
    inv_freq = f32(ROPE_THETA) ** (-np.arange(half, dtype=f32) / f32(half))
    ang = np.arange(seq, dtype=f32)[:, None] * inv_freq[None, :]
    cos = np.tile(np.cos(ang), (1, LANES // half)).astype(f32)
    sin = np.sin(ang).astype(f32)
    sin = np.concatenate([-sin, sin, -sin, sin], axis=1)
    q_scale = f32(LOG2E / math.sqrt(A_HEAD_DIM))
    return jnp.asarray(np.concatenate([cos * q_scale, sin * q_scale, cos, sin], axis=1))


def _band_bias():
    row = np.arange(QB)[:, None]
    col = np.arange(SPAN)[None, :]
    variants = []
    for off in (0, HALF_WINDOW, 2 * HALF_WINDOW):
        rel = col - off - row
        variants.append(np.where(np.abs(rel) <= HALF_WINDOW, 0.0, NEG_INF))
    variants8 = []
    for off in (0, QB):
        rel = col - off - row
        variants8.append(np.where((np.abs(rel) <= 2 * HALF_WINDOW) & (rel % 2 == 0), 0.0, NEG_INF))
    return (jnp.asarray(np.stack(variants).astype(np.float32)),
            jnp.asarray(np.stack(variants8).astype(np.float32)))


def kernel(x, norm1_g, w_in, conv_w, gate_i_b, gate_f_b, head_norm_g, w_out, norm2_g, w_up, w_down, final_g):
    batch, seq, _ = x.shape
    depth = w_in.shape[0]
    n_chunks = seq // CHUNK
    rope_tab = _rope_tables(seq)
    bias, bias8 = _band_bias()

    w_groups = _split_w_in(w_in)
    w_out_b, w_up_b, w_down_b = w_out.astype(BF16), w_up.astype(BF16), w_down.astype(BF16)
    gate_bias = jnp.concatenate([gate_i_b, gate_f_b], axis=1).reshape(depth * N_GATES).astype(F32)
    norm1_3, norm2_3, head_g3 = norm1_g[:, None, :], norm2_g[:, None, :], head_norm_g[:, None, :]

    x2d = x.reshape(batch * seq, D_MODEL)
    for layer in range(depth):
        proj, va, gates_t = _in_proj(x2d, norm1_3, w_groups, rope_tab, layer, seq)
        proj3 = proj.reshape(batch, seq, PROJ_COLS)
        ym = _mlstm(proj3, va.reshape(batch, seq, 2 * M_WIDTH), gates_t.reshape(LANES, batch, n_chunks, CHUNK),
                    gate_bias, conv_w, head_g3, layer)
        ya = _attention(proj3, bias, bias8)
        x2d = _out_mlp(x2d, ym.reshape(batch * seq, M_WIDTH), ya.reshape(batch * seq, A_WIDTH),
                       w_out_b, norm2_3, w_up_b, w_down_b, final_g[None, :],
                       layer, final=(layer == depth - 1))
    return x2d.reshape(batch, seq, D_MODEL)
```

```python
import functools
import math

import jax
import jax.numpy as jnp
import numpy as np
from jax import lax
from jax.experimental import pallas as pl
from jax.experimental.pallas import tpu as pltpu

F32 = jnp.float32
BF16 = jnp.bfloat16

D_MODEL = 1024
M_HEADS = 4
M_HEAD_DIM = 128
M_WIDTH = M_HEADS * M_HEAD_DIM
A_HEADS = 8
A_HEAD_DIM = 64
A_WIDTH = A_HEADS * A_HEAD_DIM
D_FF = 4 * D_MODEL
CONV_WIDTH = 5
ROPE_THETA = 10000.0
NORM_EPS = 1e-6
NEG_INF = -1e30
DILATIONS = (1, 4, 16)
HALF_WINDOW = 64

LANES = 128
N_GATES = 4 * M_HEADS
MAIN_COLS = 4 * M_WIDTH + 3 * A_WIDTH
PROJ_COLS = MAIN_COLS + LANES
MV_COL = 2 * M_WIDTH
OGATE_COL = 3 * M_WIDTH
AQ_COL = 4 * M_WIDTH
AK_COL = AQ_COL + A_WIDTH
CHUNK = 128
QB = 128
SPAN = QB + 2 * HALF_WINDOW
VMEM_LIMIT = 56 * 1024 * 1024
LOG2E = 1.4426950408889634
N_GATE_ROWS = 5


def _sigmoid(x):
    return 1.0 / (1.0 + jnp.exp(-x))


def _in_proj_kernel(x_ref, g_ref, wm_ref, wa_ref, wg_ref, rope_ref, o_ref, va_ref, gt_ref):
    x = x_ref[...]
    ms = jnp.mean(x * x, axis=-1, keepdims=True)
    h = (x * lax.rsqrt(ms + NORM_EPS) * g_ref[...]).astype(BF16)
    groups = ((0, wm_ref), (AQ_COL, wa_ref), (MAIN_COLS, wg_ref))
    lane = lax.broadcasted_iota(jnp.int32, (x.shape[0], LANES), 1)
    first_half = (lane % A_HEAD_DIM) < (A_HEAD_DIM // 2)
    for c in range(0, PROJ_COLS, M_WIDTH):
        w = min(M_WIDTH, PROJ_COLS - c)
        base, w_ref = [(b0, ref) for b0, ref in groups if b0 <= c][-1]
        r = jnp.dot(h, w_ref[:, c - base:c - base + w], preferred_element_type=F32)
        if c == OGATE_COL:
            r = _sigmoid(r)
        if c == MV_COL:
            for hd in range(M_HEADS):
                va_ref[:, 2 * hd * LANES:(2 * hd + 1) * LANES] = r[:, hd * LANES:(hd + 1) * LANES].astype(BF16)
                va_ref[:, (2 * hd + 1) * LANES:(2 * hd + 2) * LANES] = jnp.ones((r.shape[0], LANES), BF16)
        if c == MAIN_COLS:
            gt_ref[...] = r.T
        if c in (AQ_COL, AK_COL):
            t = 0 if c == AQ_COL else 2
            cos = rope_ref[:, t * LANES:(t + 1) * LANES]
            sin = rope_ref[:, (t + 1) * LANES:(t + 2) * LANES]
            for j in range(0, w, LANES):
                xj = r[:, j:j + LANES]
                partner = jnp.where(first_half, pltpu.roll(xj, LANES - A_HEAD_DIM // 2, axis=1),
                                    pltpu.roll(xj, A_HEAD_DIM // 2, axis=1))
                o_ref[:, c + j:c + j + LANES] = xj * cos + partner * sin
        else:
            o_ref[:, c:c + w] = r


def _in_proj(x2d, g, weights, rope_tab, layer, seq, tm=512):
    m = x2d.shape[0]
    tiles_per_seq = seq // tm
    resident = lambda a: pl.BlockSpec((None,) + a.shape[1:], lambda i: (layer, 0, 0), pipeline_mode=pl.Buffered(1))
    return pl.pallas_call(
        _in_proj_kernel,
        grid=(m // tm,),
        in_specs=[
            pl.BlockSpec((tm, D_MODEL), lambda i: (i, 0)),
            pl.BlockSpec((None, 1, D_MODEL), lambda i: (layer, 0, 0)),
            *[resident(a) for a in weights],
            pl.BlockSpec((tm, 4 * LANES), lambda i: (i % tiles_per_seq, 0)),
        ],
        out_specs=[
            pl.BlockSpec((tm, PROJ_COLS), lambda i: (i, 0)),
            pl.BlockSpec((tm, 2 * M_WIDTH), lambda i: (i, 0)),
            pl.BlockSpec((LANES, tm), lambda i: (0, i)),
        ],
        out_shape=[
            jax.ShapeDtypeStruct((m, PROJ_COLS), F32),
            jax.ShapeDtypeStruct((m, 2 * M_WIDTH), BF16),
            jax.ShapeDtypeStruct((LANES, m), F32),
        ],
        compiler_params=pltpu.CompilerParams(
            dimension_semantics=("arbitrary",), vmem_limit_bytes=VMEM_LIMIT),
        name="in_proj",
    )(x2d, g, *weights, rope_tab)


def _mlstm_kernel(bias_ref, q_ref, k_ref, va_ref, o_ref, li_f_ref, li_b_ref, lf_f_ref, lf_b_ref,
                  cwq_ref, cwk_ref, hg_ref, y_ref,
                  edge_ref, qc_ref, kc_ref, cst_ref, cprev_ref, gl_ref, *, bias_base):
    seq = q_ref.shape[0]
    n_chunks = seq // CHUNK
    head = pl.program_id(1)
    pad = CONV_WIDTH // 2

    U, NLF, MPREV, WROW, DECAY = range(N_GATE_ROWS)
    chunk_id = lax.broadcasted_iota(jnp.int32, (n_chunks, LANES), 0)
    src = lax.broadcasted_iota(jnp.int32, (LANES, LANES), 0)
    dst = lax.broadcasted_iota(jnp.int32, (LANES, LANES), 1)
    for d, (li_ref, lf_ref) in enumerate(((li_f_ref, lf_f_ref), (li_b_ref, lf_b_ref))):
        li = (li_ref[...] + bias_ref[bias_base + d * M_HEADS + head]) * LOG2E
        fpre = lf_ref[...] + bias_ref[bias_base + 2 * M_HEADS + d * M_HEADS + head]
        lf = (jnp.minimum(fpre, 0.0) - jnp.log(1.0 + jnp.exp(-jnp.abs(fpre)))) * LOG2E
        tri = jnp.where(src <= dst if d == 0 else src >= dst, 1.0, 0.0).astype(BF16)
        hi = lf.astype(BF16)
        rest = lf - hi.astype(F32)
        mid = rest.astype(BF16)
        lo = (rest - mid.astype(F32)).astype(BF16)
        parts = jnp.dot(jnp.concatenate([hi, mid, lo], axis=0), tri, preferred_element_type=F32)
        b = parts[0:n_chunks] + parts[n_chunks:2 * n_chunks] + parts[2 * n_chunks:]
        u = li - b
        tot = jnp.broadcast_to(jnp.sum(lf, axis=-1, keepdims=True), lf.shape)
        gmax = jnp.broadcast_to(jnp.max(u, axis=-1, keepdims=True), lf.shape) + tot
        m = jnp.full((1, LANES), NEG_INF, F32)
        m_prev = jnp.zeros((n_chunks, LANES), F32)
        m_new = jnp.zeros((n_chunks, LANES), F32)
        for step in range(n_chunks):
            c = step if d == 0 else n_chunks - 1 - step
            m_prev = jnp.where(chunk_id == c, m, m_prev)
            m = jnp.maximum(tot[c:c + 1, :] + m, gmax[c:c + 1, :])
            m_new = jnp.where(chunk_id == c, m, m_new)
        gl_ref[d, U] = u
        gl_ref[d, NLF] = -lf
        gl_ref[d, MPREV] = m_prev
        gl_ref[d, WROW] = jnp.exp2(u + tot - m_new)
        gl_ref[d, DECAY] = jnp.exp2(tot + m_prev - m_new)

    def conv_chunk(c, src_ref, cw_ref, dst_ref, scale, slot):
        r0 = c * CHUNK
        if c == 0:
            edge_ref[slot, 0:8, :] = jnp.zeros((8, LANES), F32)
            edge_ref[slot, 8:8 + CHUNK + 8, :] = src_ref[0:CHUNK + 8, :]
            window = lambda off: edge_ref[slot, 8 + off:8 + off + CHUNK, :]
        elif c == n_chunks - 1:
            edge_ref[slot + 1, 0:CHUNK + 8, :] = src_ref[seq - CHUNK - 8:seq, :]
            edge_ref[slot + 1, CHUNK + 8:CHUNK + 16, :] = jnp.zeros((8, LANES), F32)
            window = lambda off: edge_ref[slot + 1, 8 + off:8 + off + CHUNK, :]
        else:
            window = lambda off: src_ref[r0 + off:r0 + off + CHUNK, :]
        acc = jnp.zeros((CHUNK, LANES), F32)
        for t in range(CONV_WIDTH):
            acc = acc + cw_ref[t:t + 1, :] * window(t - pad)
        act = acc * _sigmoid(acc)
        if scale != 1.0:
            act = act * scale
        dst_ref[r0:r0 + CHUNK, :] = act.astype(dst_ref.dtype)

    q_conv = lambda c: conv_chunk(c, q_ref, cwq_ref, qc_ref, 1.0, 0)
    k_conv = lambda c: conv_chunk(c, k_ref, cwk_ref, kc_ref, 1.0 / math.sqrt(M_HEAD_DIM), 2)

    cst_ref[...] = jnp.zeros(cst_ref.shape, F32)

    def state_step(i):
        for d in range(2):
            c = i if d == 0 else n_chunks - 1 - i
            rows = pl.ds(c * CHUNK, CHUNK)
            w = gl_ref[d, WROW, c:c + 1, :]
            decay = gl_ref[d, DECAY, c:c + 1, :]
            upd = jnp.dot((kc_ref[rows, :].T * w).astype(BF16), va_ref[rows, :],
                          preferred_element_type=F32)
            state = cst_ref[d]
            cprev_ref[d, c] = state.astype(BF16)
            cst_ref[d] = jnp.concatenate([decay, decay], axis=1) * state + upd

    row = lax.broadcasted_iota(jnp.int32, (CHUNK, CHUNK), 0)
    col = lax.broadcasted_iota(jnp.int32, (CHUNK, CHUNK), 1)
    masks = (col <= row, col >= row)

    def out_chunk(c):
        rows = pl.ds(c * CHUNK, CHUNK)
        q = qc_ref[rows, :]
        va = va_ref[rows, :]
        s = lax.dot_general(q, kc_ref[rows, :].astype(BF16), (((1,), (1,)), ((), ())), preferred_element_type=F32)
        hm = jnp.zeros((CHUNK, LANES), F32)
        for d in range(2):
            mask = masks[d]
            u = gl_ref[d, U, c:c + 1, :]
            nlf = gl_ref[d, NLF, c:c + 1, :]
            m_prev = gl_ref[d, MPREV, c:c + 1, :]
            um = jnp.where(mask, u, NEG_INF)
            mloc = jnp.maximum(jnp.max(um, axis=-1, keepdims=True), m_prev)
            dt = jnp.exp2(um - mloc)
            inter = jnp.exp2(m_prev - mloc)
            lhs = jnp.concatenate([(s * dt).astype(BF16), q * inter.astype(BF16)], axis=1)
            rhs = jnp.concatenate([va, cprev_ref[d, c]], axis=0)
            out = jnp.dot(lhs, rhs, preferred_element_type=F32)
            nb_col = jnp.sum(jnp.where(mask, nlf, 0.0), axis=-1, keepdims=True)
            stab = jnp.exp2(nb_col - mloc)
            hm = hm + out[:, 0:LANES] / jnp.maximum(jnp.abs(out[:, LANES:]), stab)
        ms = jnp.mean(hm * hm, axis=-1, keepdims=True)
        y = hm * lax.rsqrt(ms + NORM_EPS) * hg_ref[...] * o_ref[rows, :]
        y_ref[rows, :] = y.astype(y_ref.dtype)

    for conv in (q_conv, k_conv):
        for c in (0, n_chunks - 1, *range(1, n_chunks - 1)):
            conv(c)
    for i in range(n_chunks):
        state_step(i)
    for c in range(n_chunks):
        out_chunk(c)


def _mlstm(proj3, va3, gates_t, gate_bias, conv_w, head_norm_g, layer):
    batch, seq, _ = proj3.shape
    n_chunks = seq // CHUNK
    col = lambda off: pl.BlockSpec((None, seq, LANES), lambda b, h: (b, 0, off + h))
    gate = lambda off: pl.BlockSpec((None, None, n_chunks, LANES), lambda b, h: (off + h, b, 0, 0))
    return pl.pallas_call(
        functools.partial(_mlstm_kernel, bias_base=layer * N_GATES),
        grid=(batch, M_HEADS),
        in_specs=[
            pl.BlockSpec(memory_space=pltpu.SMEM),
            col(0), col(M_HEADS),
            pl.BlockSpec((None, seq, 2 * LANES), lambda b, h: (b, 0, h)),
            col(OGATE_COL // LANES),
            gate(0), gate(M_HEADS), gate(2 * M_HEADS), gate(3 * M_HEADS),
            pl.BlockSpec((None, CONV_WIDTH, LANES), lambda b, h: (layer, 0, h)),
            pl.BlockSpec((None, CONV_WIDTH, LANES), lambda b, h: (layer, 0, M_HEADS + h)),
            pl.BlockSpec((None, 1, LANES), lambda b, h: (layer, 0, h)),
        ],
        out_specs=pl.BlockSpec((None, seq, LANES), lambda b, h: (b, 0, h)),
        out_shape=jax.ShapeDtypeStruct((batch, seq, M_WIDTH), BF16),
        scratch_shapes=[
            pltpu.VMEM((4, CHUNK + 16, LANES), F32),
            pltpu.VMEM((seq, LANES), BF16),
            pltpu.VMEM((seq, LANES), F32),
            pltpu.VMEM((2, M_HEAD_DIM, 2 * LANES), F32),
            pltpu.VMEM((2, n_chunks, M_HEAD_DIM, 2 * LANES), BF16),
            pltpu.VMEM((2, N_GATE_ROWS, n_chunks, LANES), F32),
        ],
        compiler_params=pltpu.CompilerParams(
            dimension_semantics=("arbitrary", "arbitrary"), vmem_limit_bytes=VMEM_LIMIT),
        name="mlstm",
    )(gate_bias, proj3, proj3, va3, proj3, gates_t, gates_t, gates_t, gates_t,
      conv_w, conv_w, head_norm_g)


def _attn_kernel(q_ref, k_ref, v_ref, bias_ref, bias8_ref, y_ref,
                 c4_ref, c8_ref, num_ref, den_ref, mx_ref):
    seq = q_ref.shape[0]
    rope_rows = 256

    d4, d16 = DILATIONS[1], DILATIONS[2]
    d8 = d16 // 2
    len4, len8 = seq // d4, seq // d8

    def split4(r, l0):
        for t, src in enumerate((q_ref, k_ref, v_ref)):
            c4_ref[t, r * len4 + l0:r * len4 + l0 + rope_rows, :] = src[pl.ds(r + d4 * l0, rope_rows, stride=d4), :]

    def split8(r):
        for t in range(3):
            c8_ref[t, r * len8:(r + 1) * len8, :] = c4_ref[t, pl.ds((r % d4) * len4 + r // d4, len8, stride=d8 // d4), :]

    lane_q = lax.broadcasted_iota(jnp.int32, (QB, LANES), 1)
    head0_q = lane_q < A_HEAD_DIM

    def block(q, k, v, bias):
        span = k.shape[0]
        q2 = jnp.concatenate([jnp.where(head0_q, q, 0.0), jnp.where(head0_q, 0.0, q)], axis=0).astype(BF16)
        s = lax.dot_general(q2, k.astype(BF16), (((1,), (1,)), ((), ())), preferred_element_type=F32)
        s = s + jnp.concatenate([bias, bias], axis=0)
        m = jnp.max(s, axis=-1, keepdims=True)
        p = jnp.exp2(s - m).astype(BF16)
        rhs = jnp.concatenate([v.astype(BF16), jnp.ones((span, LANES), BF16)], axis=1)
        acc = jnp.dot(p, rhs, preferred_element_type=F32)
        return (jnp.where(head0_q, acc[0:QB, 0:LANES], acc[QB:, 0:LANES]),
                jnp.where(head0_q, acc[0:QB, LANES:], acc[QB:, LANES:]),
                jnp.where(head0_q, m[0:QB], m[QB:2 * QB]))

    stats_refs = (num_ref, den_ref, mx_ref)

    nat = (lambda rows: q_ref[rows, :], lambda rows: k_ref[rows, :], lambda rows: v_ref[rows, :])
    cm4 = tuple((lambda rows, t=t: c4_ref[t, rows, :]) for t in range(3))
    cm8 = tuple((lambda rows, t=t: c8_ref[t, rows, :]) for t in range(3))
    patterns = ((DILATIONS[0], HALF_WINDOW, nat, bias_ref),
                (d4, HALF_WINDOW, cm4, bias_ref),
                (d8, HALF_WINDOW * (d16 // d8), cm8, bias8_ref))

    def band_block(pat, r, j):
        mod, half, srcs, biases = patterns[pat]
        length = seq // mod
        q0 = j * QB
        k0 = min(max(q0 - half, 0), length - SPAN)
        qrows = pl.ds(r * length + q0, QB)
        krows = pl.ds(r * length + k0, SPAN)
        out_rows = qrows if mod == 1 else pl.ds(r + mod * q0, QB, stride=mod)
        res = block(srcs[0](qrows), srcs[1](krows), srcs[2](krows), biases[(q0 - k0) // half])
        for ref, val in zip(stats_refs, res):
            ref[pat, out_rows, :] = val

    def merge(c):
        rows = pl.ds(c * rope_rows, rope_rows)
        m_all = jnp.maximum(jnp.maximum(mx_ref[0, rows, :], mx_ref[1, rows, :]), mx_ref[2, rows, :])
        num = jnp.zeros((rope_rows, LANES), F32)
        den = jnp.zeros((rope_rows, LANES), F32)
        for pat in range(len(DILATIONS)):
            w = jnp.exp2(mx_ref[pat, rows, :] - m_all)
            num = num + w * num_ref[pat, rows, :]
            den = den + w * den_ref[pat, rows, :]
        y_ref[rows, :] = (num / den).astype(y_ref.dtype)

    for r in range(d4):
        for l0 in range(0, len4, rope_rows):
            split4(r, l0)
    for r in range(d8):
        split8(r)
    n_groups = len8 // QB
    group_rows = seq // n_groups
    for g in range(n_groups):
        for pat in reversed(range(len(patterns))):
            mod = patterns[pat][0]
            per_group = (seq // mod // QB) // n_groups
            for r in range(mod):
                for j in range(g * per_group, (g + 1) * per_group):
                    band_block(pat, r, j)
        for c in range(g * group_rows // rope_rows, (g + 1) * group_rows // rope_rows):
            merge(c)


def _attention(proj3, bias, bias8):
    batch, seq, _ = proj3.shape
    assert DILATIONS == (1, 4, 16) and seq * 2 // DILATIONS[2] == SPAN
    n_pairs = A_WIDTH // LANES
    base = 4 * M_WIDTH // LANES
    col = lambda off: pl.BlockSpec((None, seq, LANES), lambda b, p: (b, 0, base + off + p))
    const2 = lambda shape: pl.BlockSpec(shape, lambda b, p: (0,) * len(shape))
    n_pat = len(DILATIONS)
    return pl.pallas_call(
        _attn_kernel,
        grid=(batch, n_pairs),
        in_specs=[
            col(0), col(n_pairs), col(2 * n_pairs),
            const2((3, QB, SPAN)), const2((2, QB, SPAN)),
        ],
        out_specs=pl.BlockSpec((None, seq, LANES), lambda b, p: (b, 0, p)),
        out_shape=jax.ShapeDtypeStruct((batch, seq, A_WIDTH), BF16),
        scratch_shapes=[
            pltpu.VMEM((3, seq, LANES), F32),
            pltpu.VMEM((3, seq, LANES), F32),
            pltpu.VMEM((n_pat, seq, LANES), F32),
            pltpu.VMEM((n_pat, seq, LANES), F32),
            pltpu.VMEM((n_pat, seq, LANES), F32),
        ],
        compiler_params=pltpu.CompilerParams(
            dimension_semantics=("arbitrary", "arbitrary"), vmem_limit_bytes=VMEM_LIMIT),
        name="dilated_attn",
    )(proj3, proj3, proj3, bias, bias8)


def _out_mlp_kernel(x_ref, ym_ref, ya_ref, wo_ref, g2_ref, wu_ref, wd_ref, fg_ref, o_ref, usq_ref,
                    *, ff_chunk, final):
    x1 = (x_ref[...]
          + jnp.dot(ym_ref[...], wo_ref[0:M_WIDTH, :], preferred_element_type=F32)
          + jnp.dot(ya_ref[...], wo_ref[M_WIDTH:, :], preferred_element_type=F32))
    ms = jnp.mean(x1 * x1, axis=-1, keepdims=True)
    h = (x1 * lax.rsqrt(ms + NORM_EPS) * g2_ref[...]).astype(BF16)
    for c in range(0, D_FF, ff_chunk):
        u = jnp.maximum(jnp.dot(h, wu_ref[:, c:c + ff_chunk], preferred_element_type=F32), 0.0)
        usq_ref[:, c:c + ff_chunk] = (u * u).astype(BF16)
    out = x1 + jnp.dot(usq_ref[...], wd_ref[...], preferred_element_type=F32)
    if final:
        ms = jnp.mean(out * out, axis=-1, keepdims=True)
        out = out * lax.rsqrt(ms + NORM_EPS) * fg_ref[...]
    o_ref[...] = out


def _out_mlp(x2d, ym, ya, w_out, g2, w_up, w_down, final_g, layer, final, tm=512):
    m = x2d.shape[0]
    resident = lambda shape: pl.BlockSpec((None,) + shape, lambda i: (layer, 0, 0), pipeline_mode=pl.Buffered(1))
    return pl.pallas_call(
        functools.partial(_out_mlp_kernel, ff_chunk=1024, final=final),
        grid=(m // tm,),
        in_specs=[
            pl.BlockSpec((tm, D_MODEL), lambda i: (i, 0)),
            pl.BlockSpec((tm, M_WIDTH), lambda i: (i, 0)),
            pl.BlockSpec((tm, A_WIDTH), lambda i: (i, 0)),
            resident((D_MODEL, D_MODEL)),
            resident((1, D_MODEL)),
            resident((D_MODEL, D_FF)),
            resident((D_FF, D_MODEL)),
            pl.BlockSpec((1, D_MODEL), lambda i: (0, 0), pipeline_mode=pl.Buffered(1)),
        ],
        out_specs=pl.BlockSpec((tm, D_MODEL), lambda i: (i, 0)),
        out_shape=jax.ShapeDtypeStruct((m, D_MODEL), F32),
        scratch_shapes=[pltpu.VMEM((tm, D_FF), BF16)],
        compiler_params=pltpu.CompilerParams(
            dimension_semantics=("arbitrary",), vmem_limit_bytes=VMEM_LIMIT),
        name="out_mlp",
    )(x2d, ym, ya, w_out, g2, w_up, w_down, final_g)


def _rope_tables(seq):
    half = A_HEAD_DIM // 2
    f32 = np.float32
    inv_freq = f32(ROPE_THETA) ** (-np.arange(half, dtype=f32) / f32(half))
    ang = np.arange(seq, dtype=f32)[:, None] * inv_freq[None, :]
    cos = np.tile(np.cos(ang), (1, LANES // half)).astype(f32)
    sin = np.sin(ang).astype(f32)
    sin = np.concatenate([-sin, sin, -sin, sin], axis=1)
    q_scale = f32(LOG2E / math.sqrt(A_HEAD_DIM))
    return jnp.asarray(np.concatenate([cos * q_scale, sin * q_scale, cos, sin], axis=1))


def _band_bias():
    row = np.arange(QB)[:, None]
    col = np.arange(SPAN)[None, :]
    variants = []
    for off in (0, HALF_WINDOW, 2 * HALF_WINDOW):
        rel = col - off - row
        variants.append(np.where(np.abs(rel) <= HALF_WINDOW, 0.0, NEG_INF))
    variants8 = []
    for off in (0, QB):
        rel = col - off - row
        variants8.append(np.where((np.abs(rel) <= 2 * HALF_WINDOW) & (rel % 2 == 0), 0.0, NEG_INF))
    return (jnp.asarray(np.stack(variants).astype(np.float32)),
            jnp.asarray(np.stack(variants8).astype(np.float32)))


def kernel(x, norm1_g, w_in, conv_w, gate_i_b, gate_f_b, head_norm_g, w_out, norm2_g, w_up, w_down, final_g):
    batch, seq, _ = x.shape
    depth = w_in.shape[0]
    n_chunks = seq // CHUNK
    rope_tab = _rope_tables(seq)
    bias, bias8 = _band_bias()

    gate_lo = 4 * M_WIDTH
    attn_lo = gate_lo + N_GATES
    w_gates = jnp.pad(w_in[:, :, gate_lo:attn_lo], ((0, 0), (0, 0), (0, LANES - N_GATES)))
    w_groups = tuple(a.astype(BF16) for a in (w_in[:, :, :gate_lo], w_in[:, :, attn_lo:], w_gates))
    w_out_b, w_up_b, w_down_b = w_out.astype(BF16), w_up.astype(BF16), w_down.astype(BF16)
    gate_bias = jnp.concatenate([gate_i_b, gate_f_b], axis=1).reshape(depth * N_GATES).astype(F32)
    norm1_3, norm2_3, head_g3 = norm1_g[:, None, :], norm2_g[:, None, :], head_norm_g[:, None, :]

    x2d = x.reshape(batch * seq, D_MODEL)
    for layer in range(depth):
        proj, va, gates_t = _in_proj(x2d, norm1_3, w_groups, rope_tab, layer, seq)
        proj3 = proj.reshape(batch, seq, PROJ_COLS)
        ym = _mlstm(proj3, va.reshape(batch, seq, 2 * M_WIDTH), gates_t.reshape(LANES, batch, n_chunks, CHUNK),
                    gate_bias, conv_w, head_g3, layer)
        ya = _attention(proj3, bias, bias8)
        x2d = _out_mlp(x2d, ym.reshape(batch * seq, M_WIDTH), ya.reshape(batch * seq, A_WIDTH),
                       w_out_b, norm2_3, w_up_b, w_down_b, final_g[None, :],
                       layer, final=(layer == depth - 1))
    return x2d.reshape(batch, seq, D_MODEL)
```

```python
import functools
import math

import jax
import jax.numpy as jnp
import numpy as np
from jax import lax
from jax.experimental import pallas as pl
from jax.experimental.pallas import tpu as pltpu

F32 = jnp.float32
BF16 = jnp.bfloat16

D_MODEL = 1024
M_HEADS = 4
M_HEAD_DIM = 128
M_WIDTH = M_HEADS * M_HEAD_DIM
A_HEADS = 8
A_HEAD_DIM = 64
A_WIDTH = A_HEADS * A_HEAD_DIM
D_FF = 4 * D_MODEL
CONV_WIDTH = 5
ROPE_THETA = 10000.0
NORM_EPS = 1e-6
NEG_INF = -1e30
DILATIONS = (1, 4, 16)
HALF_WINDOW = 64

LANES = 128
N_GATES = 4 * M_HEADS
MAIN_COLS = 4 * M_WIDTH + 3 * A_WIDTH
PROJ_COLS = MAIN_COLS + LANES
MV_COL = 2 * M_WIDTH
OGATE_COL = 3 * M_WIDTH
AQ_COL = 4 * M_WIDTH
AK_COL = AQ_COL + A_WIDTH
CHUNK = 128
QB = 128
SPAN = QB + 2 * HALF_WINDOW
VMEM_LIMIT = 56 * 1024 * 1024
LOG2E = 1.4426950408889634
N_GATE_ROWS = 5


def _sigmoid(x):
    return 1.0 / (1.0 + jnp.exp(-x))


def _in_proj_kernel(x_ref, g_ref, hg_ref, wm_ref, wa_ref, wg_ref, rope_ref, o_ref, va_ref, gt_ref):
    x = x_ref[...]
    ms = jnp.mean(x * x, axis=-1, keepdims=True)
    h = (x * lax.rsqrt(ms + NORM_EPS) * g_ref[...]).astype(BF16)
    groups = ((0, wm_ref), (AQ_COL, wa_ref), (MAIN_COLS, wg_ref))
    lane = lax.broadcasted_iota(jnp.int32, (x.shape[0], LANES), 1)
    first_half = (lane % A_HEAD_DIM) < (A_HEAD_DIM // 2)
    for c in range(0, PROJ_COLS, M_WIDTH):
        w = min(M_WIDTH, PROJ_COLS - c)
        base, w_ref = [(b0, ref) for b0, ref in groups if b0 <= c][-1]
        r = jnp.dot(h, w_ref[:, c - base:c - base + w], preferred_element_type=F32)
        if c == OGATE_COL:
            r = _sigmoid(r) * hg_ref[...]
        if c == MV_COL:
            for hd in range(M_HEADS):
                va_ref[:, 2 * hd * LANES:(2 * hd + 1) * LANES] = r[:, hd * LANES:(hd + 1) * LANES].astype(BF16)
                va_ref[:, (2 * hd + 1) * LANES:(2 * hd + 2) * LANES] = jnp.ones((r.shape[0], LANES), BF16)
        if c == MAIN_COLS:
            gt_ref[...] = r.T
        if c in (AQ_COL, AK_COL):
            t = 0 if c == AQ_COL else 2
            cos = rope_ref[:, t * LANES:(t + 1) * LANES]
            sin = rope_ref[:, (t + 1) * LANES:(t + 2) * LANES]
            for j in range(0, w, LANES):
                xj = r[:, j:j + LANES]
                partner = jnp.where(first_half, pltpu.roll(xj, LANES - A_HEAD_DIM // 2, axis=1),
                                    pltpu.roll(xj, A_HEAD_DIM // 2, axis=1))
                o_ref[:, c + j:c + j + LANES] = xj * cos + partner * sin
        else:
            o_ref[:, c:c + w] = r


def _in_proj(x2d, g, head_norm_g, weights, rope_tab, layer, seq, tm=512):
    m = x2d.shape[0]
    tiles_per_seq = seq // tm
    resident = lambda a: pl.BlockSpec((None,) + a.shape[1:], lambda i: (layer, 0, 0), pipeline_mode=pl.Buffered(1))
    return pl.pallas_call(
        _in_proj_kernel,
        grid=(m // tm,),
        in_specs=[
            pl.BlockSpec((tm, D_MODEL), lambda i: (i, 0)),
            pl.BlockSpec((None, 1, D_MODEL), lambda i: (layer, 0, 0)),
            pl.BlockSpec((None, 1, M_WIDTH), lambda i: (layer, 0, 0)),
            *[resident(a) for a in weights],
            pl.BlockSpec((tm, 4 * LANES), lambda i: (i % tiles_per_seq, 0)),
        ],
        out_specs=[
            pl.BlockSpec((tm, PROJ_COLS), lambda i: (i, 0)),
            pl.BlockSpec((tm, 2 * M_WIDTH), lambda i: (i, 0)),
            pl.BlockSpec((LANES, tm), lambda i: (0, i)),
        ],
        out_shape=[
            jax.ShapeDtypeStruct((m, PROJ_COLS), F32),
            jax.ShapeDtypeStruct((m, 2 * M_WIDTH), BF16),
            jax.ShapeDtypeStruct((LANES, m), F32),
        ],
        compiler_params=pltpu.CompilerParams(
            dimension_semantics=("arbitrary",), vmem_limit_bytes=VMEM_LIMIT),
        name="in_proj",
    )(x2d, g, head_norm_g, *weights, rope_tab)


def _mlstm_kernel(bias_ref, q_ref, k_ref, va_ref, o_ref, li_f_ref, li_b_ref, lf_f_ref, lf_b_ref,
                  cwq_ref, cwk_ref, y_ref,
                  edge_ref, qc_ref, kc_ref, cst_ref, cprev_ref, gl_ref, *, bias_base):
    seq = q_ref.shape[0]
    n_chunks = seq // CHUNK
    head = pl.program_id(1)
    pad = CONV_WIDTH // 2

    U, NLF, MPREV, WROW, DECAY = range(N_GATE_ROWS)
    chunk_id = lax.broadcasted_iota(jnp.int32, (n_chunks, LANES), 0)
    src = lax.broadcasted_iota(jnp.int32, (LANES, LANES), 0)
    dst = lax.broadcasted_iota(jnp.int32, (LANES, LANES), 1)
    for d, (li_ref, lf_ref) in enumerate(((li_f_ref, lf_f_ref), (li_b_ref, lf_b_ref))):
        li = (li_ref[...] + bias_ref[bias_base + d * M_HEADS + head]) * LOG2E
        fpre = lf_ref[...] + bias_ref[bias_base + 2 * M_HEADS + d * M_HEADS + head]
        lf = (jnp.minimum(fpre, 0.0) - jnp.log(1.0 + jnp.exp(-jnp.abs(fpre)))) * LOG2E
        tri = jnp.where(src <= dst if d == 0 else src >= dst, 1.0, 0.0).astype(BF16)
        hi = lf.astype(BF16)
        rest = lf - hi.astype(F32)
        mid = rest.astype(BF16)
        lo = (rest - mid.astype(F32)).astype(BF16)
        parts = jnp.dot(jnp.concatenate([hi, mid, lo], axis=0), tri, preferred_element_type=F32)
        b = parts[0:n_chunks] + parts[n_chunks:2 * n_chunks] + parts[2 * n_chunks:]
        u = li - b
        tot = jnp.broadcast_to(jnp.sum(lf, axis=-1, keepdims=True), lf.shape)
        gmax = jnp.broadcast_to(jnp.max(u, axis=-1, keepdims=True), lf.shape) + tot
        m = jnp.full((1, LANES), NEG_INF, F32)
        m_prev = jnp.zeros((n_chunks, LANES), F32)
        m_new = jnp.zeros((n_chunks, LANES), F32)
        for step in range(n_chunks):
            c = step if d == 0 else n_chunks - 1 - step
            m_prev = jnp.where(chunk_id == c, m, m_prev)
            m = jnp.maximum(tot[c:c + 1, :] + m, gmax[c:c + 1, :])
            m_new = jnp.where(chunk_id == c, m, m_new)
        gl_ref[d, U] = u
        gl_ref[d, NLF] = -lf
        gl_ref[d, MPREV] = m_prev
        gl_ref[d, WROW] = jnp.exp2(u + tot - m_new)
        gl_ref[d, DECAY] = jnp.exp2(tot + m_prev - m_new)

    def conv_chunk(c, src_ref, half_taps, dst_ref, scale, slot):
        r0 = c * CHUNK
        if c == 0:
            edge_ref[slot, 0:8, :] = jnp.zeros((8, LANES), F32)
            edge_ref[slot, 8:8 + CHUNK + 8, :] = src_ref[0:CHUNK + 8, :]
            window = lambda off: edge_ref[slot, 8 + off:8 + off + CHUNK, :]
        elif c == n_chunks - 1:
            edge_ref[slot + 1, 0:CHUNK + 8, :] = src_ref[seq - CHUNK - 8:seq, :]
            edge_ref[slot + 1, CHUNK + 8:CHUNK + 16, :] = jnp.zeros((8, LANES), F32)
            window = lambda off: edge_ref[slot + 1, 8 + off:8 + off + CHUNK, :]
        else:
            window = lambda off: src_ref[r0 + off:r0 + off + CHUNK, :]
        h = half_taps[0:1, :] * window(-pad)
        for t in range(1, CONV_WIDTH):
            h = h + half_taps[t:t + 1, :] * window(t - pad)
        act = h + h * jnp.tanh(h)
        if scale != 1.0:
            act = act * scale
        dst_ref[r0:r0 + CHUNK, :] = act.astype(dst_ref.dtype)

    half_q, half_k = 0.5 * cwq_ref[...], 0.5 * cwk_ref[...]
    q_conv = lambda c: conv_chunk(c, q_ref, half_q, qc_ref, 1.0, 0)
    k_conv = lambda c: conv_chunk(c, k_ref, half_k, kc_ref, 1.0 / math.sqrt(M_HEAD_DIM), 2)

    cst_ref[...] = jnp.zeros(cst_ref.shape, F32)

    def state_step(i):
        for d in range(2):
            c = i if d == 0 else n_chunks - 1 - i
            rows = pl.ds(c * CHUNK, CHUNK)
            w = gl_ref[d, WROW, c:c + 1, :]
            decay = gl_ref[d, DECAY, c:c + 1, :]
            upd = jnp.dot((kc_ref[rows, :].T * w).astype(BF16), va_ref[rows, :],
                          preferred_element_type=F32)
            state = cst_ref[d]
            cprev_ref[d, c] = state.astype(BF16)
            cst_ref[d] = jnp.concatenate([decay, decay], axis=1) * state + upd

    row = lax.broadcasted_iota(jnp.int32, (CHUNK, CHUNK), 0)
    col = lax.broadcasted_iota(jnp.int32, (CHUNK, CHUNK), 1)
    masks = (col <= row, col >= row)

    def out_chunk(c):
        rows = pl.ds(c * CHUNK, CHUNK)
        q = qc_ref[rows, :]
        va = va_ref[rows, :]
        s = lax.dot_general(q, kc_ref[rows, :].astype(BF16), (((1,), (1,)), ((), ())), preferred_element_type=F32)
        hm = jnp.zeros((CHUNK, LANES), F32)
        for d in range(2):
            mask = masks[d]
            u = gl_ref[d, U, c:c + 1, :]
            nlf = gl_ref[d, NLF, c:c + 1, :]
            m_prev = gl_ref[d, MPREV, c:c + 1, :]
            um = jnp.where(mask, u, NEG_INF)
            mloc = jnp.maximum(jnp.max(um, axis=-1, keepdims=True), m_prev)
            dt = jnp.exp2(um - mloc)
            inter = jnp.exp2(m_prev - mloc)
            lhs = jnp.concatenate([(s * dt).astype(BF16), q * inter.astype(BF16)], axis=1)
            rhs = jnp.concatenate([va, cprev_ref[d, c]], axis=0)
            out = jnp.dot(lhs, rhs, preferred_element_type=F32)
            nb_col = jnp.sum(jnp.where(mask, nlf, 0.0), axis=-1, keepdims=True)
            stab = jnp.exp2(nb_col - mloc)
            hm = hm + out[:, 0:LANES] / jnp.maximum(jnp.abs(out[:, LANES:]), stab)
        ms = jnp.mean(hm * hm, axis=-1, keepdims=True)
        y = hm * lax.rsqrt(ms + NORM_EPS) * o_ref[rows, :]
        y_ref[rows, :] = y.astype(y_ref.dtype)

    for conv in (q_conv, k_conv):
        for c in (0, n_chunks - 1, *range(1, n_chunks - 1)):
            conv(c)
    for i in range(n_chunks):
        state_step(i)
    for c in range(n_chunks):
        out_chunk(c)


def _mlstm(proj3, va3, gates_t, gate_bias, conv_w, layer):
    batch, seq, _ = proj3.shape
    n_chunks = seq // CHUNK
    col = lambda off: pl.BlockSpec((None, seq, LANES), lambda b, h: (b, 0, off + h))
    gate = lambda off: pl.BlockSpec((None, None, n_chunks, LANES), lambda b, h: (off + h, b, 0, 0))
    return pl.pallas_call(
        functools.partial(_mlstm_kernel, bias_base=layer * N_GATES),
        grid=(batch, M_HEADS),
        in_specs=[
            pl.BlockSpec(memory_space=pltpu.SMEM),
            col(0), col(M_HEADS),
            pl.BlockSpec((None, seq, 2 * LANES), lambda b, h: (b, 0, h)),
            col(OGATE_COL // LANES),
            gate(0), gate(M_HEADS), gate(2 * M_HEADS), gate(3 * M_HEADS),
            pl.BlockSpec((None, CONV_WIDTH, LANES), lambda b, h: (layer, 0, h)),
            pl.BlockSpec((None, CONV_WIDTH, LANES), lambda b, h: (layer, 0, M_HEADS + h)),
        ],
        out_specs=pl.BlockSpec((None, seq, LANES), lambda b, h: (b, 0, h)),
        out_shape=jax.ShapeDtypeStruct((batch, seq, M_WIDTH), BF16),
        scratch_shapes=[
            pltpu.VMEM((4, CHUNK + 16, LANES), F32),
            pltpu.VMEM((seq, LANES), BF16),
            pltpu.VMEM((seq, LANES), F32),
            pltpu.VMEM((2, M_HEAD_DIM, 2 * LANES), F32),
            pltpu.VMEM((2, n_chunks, M_HEAD_DIM, 2 * LANES), BF16),
            pltpu.VMEM((2, N_GATE_ROWS, n_chunks, LANES), F32),
        ],
        compiler_params=pltpu.CompilerParams(
            dimension_semantics=("arbitrary", "arbitrary"), vmem_limit_bytes=VMEM_LIMIT),
        name="mlstm",
    )(gate_bias, proj3, proj3, va3, proj3, gates_t, gates_t, gates_t, gates_t,
      conv_w, conv_w)


def _attn_kernel(q_ref, k_ref, v_ref, bias_ref, bias8_ref, y_ref,
                 c4_ref, c8_ref, num_ref, den_ref, mx_ref):
    seq = q_ref.shape[0]
    rope_rows = 256

    d4, d16 = DILATIONS[1], DILATIONS[2]
    d8 = d16 // 2
    len4, len8 = seq // d4, seq // d8

    def split4(r, l0):
        for t, src in enumerate((q_ref, k_ref, v_ref)):
            c4_ref[t, r * len4 + l0:r * len4 + l0 + rope_rows, :] = src[pl.ds(r + d4 * l0, rope_rows, stride=d4), :]

    def split8(r):
        for t in range(3):
            c8_ref[t, r * len8:(r + 1) * len8, :] = c4_ref[t, pl.ds((r % d4) * len4 + r // d4, len8, stride=d8 // d4), :]

    lane_q = lax.broadcasted_iota(jnp.int32, (QB, LANES), 1)
    head0_q = lane_q < A_HEAD_DIM

    def block(q, k, v, bias):
        span = k.shape[0]
        q2 = jnp.concatenate([jnp.where(head0_q, q, 0.0), jnp.where(head0_q, 0.0, q)], axis=0).astype(BF16)
        s = lax.dot_general(q2, k.astype(BF16), (((1,), (1,)), ((), ())), preferred_element_type=F32)
        s = s + jnp.concatenate([bias, bias], axis=0)
        m = jnp.max(s, axis=-1, keepdims=True)
        p = jnp.exp2(s - m).astype(BF16)
        rhs = jnp.concatenate([v.astype(BF16), jnp.ones((span, LANES), BF16)], axis=1)
        acc = jnp.dot(p, rhs, preferred_element_type=F32)
        return (jnp.where(head0_q, acc[0:QB, 0:LANES], acc[QB:, 0:LANES]),
                jnp.where(head0_q, acc[0:QB, LANES:], acc[QB:, LANES:]),
                jnp.where(head0_q, m[0:QB], m[QB:2 * QB]))

    stats_refs = (num_ref, den_ref, mx_ref)

    nat = (lambda rows: q_ref[rows, :], lambda rows: k_ref[rows, :], lambda rows: v_ref[rows, :])
    cm4 = tuple((lambda rows, t=t: c4_ref[t, rows, :]) for t in range(3))
    cm8 = tuple((lambda rows, t=t: c8_ref[t, rows, :]) for t in range(3))
    patterns = ((DILATIONS[0], HALF_WINDOW, nat, bias_ref),
                (d4, HALF_WINDOW, cm4, bias_ref),
                (d8, HALF_WINDOW * (d16 // d8), cm8, bias8_ref))

    def band_block(pat, r, j):
        mod, half, srcs, biases = patterns[pat]
        length = seq // mod
        q0 = j * QB
        k0 = min(max(q0 - half, 0), length - SPAN)
        qrows = pl.ds(r * length + q0, QB)
        krows = pl.ds(r * length + k0, SPAN)
        out_rows = qrows if mod == 1 else pl.ds(r + mod * q0, QB, stride=mod)
        res = block(srcs[0](qrows), srcs[1](krows), srcs[2](krows), biases[(q0 - k0) // half])
        for ref, val in zip(stats_refs, res):
            ref[pat, out_rows, :] = val

    def merge(c):
        rows = pl.ds(c * rope_rows, rope_rows)
        m_all = jnp.maximum(jnp.maximum(mx_ref[0, rows, :], mx_ref[1, rows, :]), mx_ref[2, rows, :])
        num = jnp.zeros((rope_rows, LANES), F32)
        den = jnp.zeros((rope_rows, LANES), F32)
        for pat in range(len(DILATIONS)):
            w = jnp.exp2(mx_ref[pat, rows, :] - m_all)
            num = num + w * num_ref[pat, rows, :]
            den = den + w * den_ref[pat, rows, :]
        y_ref[rows, :] = (num / den).astype(y_ref.dtype)

    for r in range(d4):
        for l0 in range(0, len4, rope_rows):
            split4(r, l0)
    for r in range(d8):
        split8(r)
    n_groups = len8 // QB
    group_rows = seq // n_groups
    for g in range(n_groups):
        for pat in reversed(range(len(patterns))):
            mod = patterns[pat][0]
            per_group = (seq // mod // QB) // n_groups
            for r in range(mod):
                for j in range(g * per_group, (g + 1) * per_group):
                    band_block(pat, r, j)
        for c in range(g * group_rows // rope_rows, (g + 1) * group_rows // rope_rows):
            merge(c)


def _attention(proj3, bias, bias8):
    batch, seq, _ = proj3.shape
    assert DILATIONS == (1, 4, 16) and seq * 2 // DILATIONS[2] == SPAN
    n_pairs = A_WIDTH // LANES
    base = 4 * M_WIDTH // LANES
    col = lambda off: pl.BlockSpec((None, seq, LANES), lambda b, p: (b, 0, base + off + p))
    const2 = lambda shape: pl.BlockSpec(shape, lambda b, p: (0,) * len(shape))
    n_pat = len(DILATIONS)
    return pl.pallas_call(
        _attn_kernel,
        grid=(batch, n_pairs),
        in_specs=[
            col(0), col(n_pairs), col(2 * n_pairs),
            const2((3, QB, SPAN)), const2((2, QB, SPAN)),
        ],
        out_specs=pl.BlockSpec((None, seq, LANES), lambda b, p: (b, 0, p)),
        out_shape=jax.ShapeDtypeStruct((batch, seq, A_WIDTH), BF16),
        scratch_shapes=[
            pltpu.VMEM((3, seq, LANES), F32),
            pltpu.VMEM((3, seq, LANES), F32),
            pltpu.VMEM((n_pat, seq, LANES), F32),
            pltpu.VMEM((n_pat, seq, LANES), F32),
            pltpu.VMEM((n_pat, seq, LANES), F32),
        ],
        compiler_params=pltpu.CompilerParams(
            dimension_semantics=("arbitrary", "arbitrary"), vmem_limit_bytes=VMEM_LIMIT),
        name="dilated_attn",
    )(proj3, proj3, proj3, bias, bias8)


def _out_mlp_kernel(x_ref, ym_ref, ya_ref, wo_ref, g2_ref, wu_ref, wd_ref, fg_ref, o_ref, usq_ref,
                    *, ff_chunk, final):
    x1 = (x_ref[...]
          + jnp.dot(ym_ref[...], wo_ref[0:M_WIDTH, :], preferred_element_type=F32)
          + jnp.dot(ya_ref[...], wo_ref[M_WIDTH:, :], preferred_element_type=F32))
    ms = jnp.mean(x1 * x1, axis=-1, keepdims=True)
    h = (x1 * lax.rsqrt(ms + NORM_EPS) * g2_ref[...]).astype(BF16)
    for c in range(0, D_FF, ff_chunk):
        u = jnp.maximum(jnp.dot(h, wu_ref[:, c:c + ff_chunk], preferred_element_type=F32), 0.0)
        usq_ref[:, c:c + ff_chunk] = (u * u).astype(BF16)
    out = x1 + jnp.dot(usq_ref[...], wd_ref[...], preferred_element_type=F32)
    if final:
        ms = jnp.mean(out * out, axis=-1, keepdims=True)
        out = out * lax.rsqrt(ms + NORM_EPS) * fg_ref[...]
    o_ref[...] = out


def _out_mlp(x2d, ym, ya, w_out, g2, w_up, w_down, final_g, layer, final, tm=512):
    m = x2d.shape[0]
    resident = lambda shape: pl.BlockSpec((None,) + shape, lambda i: (layer, 0, 0), pipeline_mode=pl.Buffered(1))
    return pl.pallas_call(
        functools.partial(_out_mlp_kernel, ff_chunk=1024, final=final),
        grid=(m // tm,),
        in_specs=[
            pl.BlockSpec((tm, D_MODEL), lambda i: (i, 0)),
            pl.BlockSpec((tm, M_WIDTH), lambda i: (i, 0)),
            pl.BlockSpec((tm, A_WIDTH), lambda i: (i, 0)),
            resident((D_MODEL, D_MODEL)),
            resident((1, D_MODEL)),
            resident((D_MODEL, D_FF)),
            resident((D_FF, D_MODEL)),
            pl.BlockSpec((1, D_MODEL), lambda i: (0, 0), pipeline_mode=pl.Buffered(1)),
        ],
        out_specs=pl.BlockSpec((tm, D_MODEL), lambda i: (i, 0)),
        out_shape=jax.ShapeDtypeStruct((m, D_MODEL), F32),
        scratch_shapes=[pltpu.VMEM((tm, D_FF), BF16)],
        compiler_params=pltpu.CompilerParams(
            dimension_semantics=("arbitrary",), vmem_limit_bytes=VMEM_LIMIT),
        name="out_mlp",
    )(x2d, ym, ya, w_out, g2, w_up, w_down, final_g)


def _rope_tables(seq):
    half = A_HEAD_DIM // 2
    f32 = np.float32
    inv_freq = f32(ROPE_THETA) ** (-np.arange(half, dtype=f32) / f32(half))
    ang = np.arange(seq, dtype=f32)[:, None] * inv_freq[None, :]
    cos = np.tile(np.cos(ang), (1, LANES // half)).astype(f32)
    sin = np.sin(ang).astype(f32)
    sin = np.concatenate([-sin, sin, -sin, sin], axis=1)
    q_scale = f32(LOG2E / math.sqrt(A_HEAD_DIM))
    return jnp.asarray(np.concatenate([cos * q_scale, sin * q_scale, cos, sin], axis=1))


def _band_bias():
    row = np.arange(QB)[:, None]
    col = np.arange(SPAN)[None, :]
    variants = []
    for off in (0, HALF_WINDOW, 2 * HALF_WINDOW):
        rel = col - off - row
        variants.append(np.where(np.abs(rel) <= HALF_WINDOW, 0.0, NEG_INF))
    variants8 = []
    for off in (0, QB):
        rel = col - off - row
        variants8.append(np.where((np.abs(rel) <= 2 * HALF_WINDOW) & (rel % 2 == 0), 0.0, NEG_INF))
    return (jnp.asarray(np.stack(variants).astype(np.float32)),
            jnp.asarray(np.stack(variants8).astype(np.float32)))


def kernel(x, norm1_g, w_in, conv_w, gate_i_b, gate_f_b, head_norm_g, w_out, norm2_g, w_up, w_down, final_g):
    batch, seq, _ = x.shape
    depth = w_in.shape[0]
    n_chunks = seq // CHUNK
    rope_tab = _rope_tables(seq)
    bias, bias8 = _band_bias()

    gate_lo = 4 * M_WIDTH
    attn_lo = gate_lo + N_GATES
    w_gates = jnp.pad(w_in[:, :, gate_lo:attn_lo], ((0, 0), (0, 0), (0, LANES - N_GATES)))
    w_groups = tuple(a.astype(BF16) for a in (w_in[:, :, :gate_lo], w_in[:, :, attn_lo:], w_gates))
    w_out_b, w_up_b, w_down_b = w_out.astype(BF16), w_up.astype(BF16), w_down.astype(BF16)
    gate_bias = jnp.concatenate([gate_i_b, gate_f_b], axis=1).reshape(depth * N_GATES).astype(F32)
    norm1_3, norm2_3, head_g3 = norm1_g[:, None, :], norm2_g[:, None, :], head_norm_g[:, None, :]

    x2d = x.reshape(batch * seq, D_MODEL)
    for layer in range(depth):
        proj, va, gates_t = _in_proj(x2d, norm1_3, head_g3, w_groups, rope_tab, layer, seq)
        proj3 = proj.reshape(batch, seq, PROJ_COLS)
        ym = _mlstm(proj3, va.reshape(batch, seq, 2 * M_WIDTH), gates_t.reshape(LANES, batch, n_chunks, CHUNK),
                    gate_bias, conv_w, layer)
        ya = _attention(proj3, bias, bias8)
        x2d = _out_mlp(x2d, ym.reshape(batch * seq, M_WIDTH), ya.reshape(batch * seq, A_WIDTH),
                       w_out_b, norm2_3, w_up_b, w_down_b, final_g[None, :],
                       layer, final=(layer == depth - 1))
    return x2d.reshape(batch, seq, D_MODEL)
```

```python
import functools
import math

import jax
import jax.numpy as jnp
import numpy as np
from jax import lax
from jax.experimental import pallas as pl
from jax.experimental.pallas import tpu as pltpu

F32 = jnp.float32
BF16 = jnp.bfloat16

D_MODEL = 1024
M_HEADS = 4
M_HEAD_DIM = 128
M_WIDTH = M_HEADS * M_HEAD_DIM
A_HEADS = 8
A_HEAD_DIM = 64
A_WIDTH = A_HEADS * A_HEAD_DIM
D_FF = 4 * D_MODEL
CONV_WIDTH = 5
ROPE_THETA = 10000.0
NORM_EPS = 1e-6
NEG_INF = -1e30
DILATIONS = (1, 4, 16)
HALF_WINDOW = 64

LANES = 128
N_GATES = 4 * M_HEADS
MAIN_COLS = 4 * M_WIDTH + 3 * A_WIDTH
PROJ_COLS = MAIN_COLS + LANES
PROJ_SLABS = M_WIDTH // LANES
N_PROJ_TENSORS = 6
SLAB_MQ, SLAB_MK, SLAB_MO, SLAB_AQ, SLAB_AK, SLAB_AV = (t * PROJ_SLABS for t in range(N_PROJ_TENSORS))
MV_COL = 2 * M_WIDTH
OGATE_COL = 3 * M_WIDTH
AQ_COL = 4 * M_WIDTH
AK_COL = AQ_COL + A_WIDTH
CHUNK = 128
QB = 128
SPAN = QB + 2 * HALF_WINDOW
VMEM_LIMIT = 56 * 1024 * 1024
LOG2E = 1.4426950408889634
N_GATE_ROWS = 5


def _sigmoid(x):
    return 1.0 / (1.0 + jnp.exp(-x))


def _in_proj_kernel(x_ref, g_ref, hg_ref, wm_ref, wa_ref, wg_ref, rope_ref, o_ref, va_ref, gt_ref):
    x = x_ref[...]
    ms = jnp.mean(x * x, axis=-1, keepdims=True)
    h = (x * lax.rsqrt(ms + NORM_EPS) * g_ref[...]).astype(BF16)
    groups = ((0, wm_ref), (AQ_COL, wa_ref), (MAIN_COLS, wg_ref))
    lane = lax.broadcasted_iota(jnp.int32, (x.shape[0], LANES), 1)
    first_half = (lane % A_HEAD_DIM) < (A_HEAD_DIM // 2)
    for c in range(0, PROJ_COLS, M_WIDTH):
        w = min(M_WIDTH, PROJ_COLS - c)
        base, w_ref = [(b0, ref) for b0, ref in groups if b0 <= c][-1]
        r = jnp.dot(h, w_ref[:, c - base:c - base + w], preferred_element_type=F32)
        if c == MAIN_COLS:
            gt_ref[...] = r.T
            continue
        if c == MV_COL:
            for hd in range(M_HEADS):
                va_ref[hd, :, 0:LANES] = r[:, hd * LANES:(hd + 1) * LANES].astype(BF16)
                va_ref[hd, :, LANES:2 * LANES] = jnp.ones((r.shape[0], LANES), BF16)
            continue
        if c == OGATE_COL:
            r = _sigmoid(r) * hg_ref[...]
        slab = PROJ_SLABS * (c // M_WIDTH - (1 if c > MV_COL else 0))
        for j in range(PROJ_SLABS):
            xj = r[:, j * LANES:(j + 1) * LANES]
            if c in (AQ_COL, AK_COL):
                t = 0 if c == AQ_COL else 2
                cos = rope_ref[:, t * LANES:(t + 1) * LANES]
                sin = rope_ref[:, (t + 1) * LANES:(t + 2) * LANES]
                partner = jnp.where(first_half, pltpu.roll(xj, LANES - A_HEAD_DIM // 2, axis=1),
                                    pltpu.roll(xj, A_HEAD_DIM // 2, axis=1))
                xj = xj * cos + partner * sin
            o_ref[slab + j] = xj


def _in_proj(x2d, g, head_norm_g, weights, rope_tab, layer, seq, tm=512):
    m = x2d.shape[0]
    tiles_per_seq = seq // tm
    resident = lambda a: pl.BlockSpec((None,) + a.shape[1:], lambda i: (layer, 0, 0), pipeline_mode=pl.Buffered(1))
    return pl.pallas_call(
        _in_proj_kernel,
        grid=(m // tm,),
        in_specs=[
            pl.BlockSpec((tm, D_MODEL), lambda i: (i, 0)),
            pl.BlockSpec((None, 1, D_MODEL), lambda i: (layer, 0, 0)),
            pl.BlockSpec((None, 1, M_WIDTH), lambda i: (layer, 0, 0)),
            *[resident(a) for a in weights],
            pl.BlockSpec((tm, 4 * LANES), lambda i: (i % tiles_per_seq, 0)),
        ],
        out_specs=[
            pl.BlockSpec((N_PROJ_TENSORS * PROJ_SLABS, tm, LANES), lambda i: (0, i, 0)),
            pl.BlockSpec((M_HEADS, tm, 2 * LANES), lambda i: (0, i, 0)),
            pl.BlockSpec((LANES, tm), lambda i: (0, i)),
        ],
        out_shape=[
            jax.ShapeDtypeStruct((N_PROJ_TENSORS * PROJ_SLABS, m, LANES), F32),
            jax.ShapeDtypeStruct((M_HEADS, m, 2 * LANES), BF16),
            jax.ShapeDtypeStruct((LANES, m), F32),
        ],
        compiler_params=pltpu.CompilerParams(
            dimension_semantics=("arbitrary",), vmem_limit_bytes=VMEM_LIMIT),
        name="in_proj",
    )(x2d, g, head_norm_g, *weights, rope_tab)


def _mlstm_kernel(bias_ref, q_ref, k_ref, va_ref, o_ref, gates_ref,
                  cwq_ref, cwk_ref, y_ref,
                  edge_ref, qc_ref, kc_ref, cst_ref, cprev_ref, gl_ref, *, bias_base):
    seq = q_ref.shape[0]
    n_chunks = seq // CHUNK
    head = pl.program_id(1)
    pad = CONV_WIDTH // 2

    U, NLF, MPREV, WROW, DECAY = range(N_GATE_ROWS)
    chunk_id = lax.broadcasted_iota(jnp.int32, (n_chunks, LANES), 0)
    src = lax.broadcasted_iota(jnp.int32, (LANES, LANES), 0)
    dst = lax.broadcasted_iota(jnp.int32, (LANES, LANES), 1)
    for d in range(2):
        li = (gates_ref[d] + bias_ref[bias_base + d * M_HEADS + head]) * LOG2E
        fpre = gates_ref[2 + d] + bias_ref[bias_base + 2 * M_HEADS + d * M_HEADS + head]
        lf = (jnp.minimum(fpre, 0.0) - jnp.log(1.0 + jnp.exp(-jnp.abs(fpre)))) * LOG2E
        tri = jnp.where(src <= dst if d == 0 else src >= dst, 1.0, 0.0).astype(BF16)
        hi = lf.astype(BF16)
        rest = lf - hi.astype(F32)
        mid = rest.astype(BF16)
        lo = (rest - mid.astype(F32)).astype(BF16)
        parts = jnp.dot(jnp.concatenate([hi, mid, lo], axis=0), tri, preferred_element_type=F32)
        b = parts[0:n_chunks] + parts[n_chunks:2 * n_chunks] + parts[2 * n_chunks:]
        u = li - b
        tot = jnp.broadcast_to(jnp.sum(lf, axis=-1, keepdims=True), lf.shape)
        gmax = jnp.broadcast_to(jnp.max(u, axis=-1, keepdims=True), lf.shape) + tot
        m = jnp.full((1, LANES), NEG_INF, F32)
        m_prev = jnp.zeros((n_chunks, LANES), F32)
        m_new = jnp.zeros((n_chunks, LANES), F32)
        for step in range(n_chunks):
            c = step if d == 0 else n_chunks - 1 - step
            m_prev = jnp.where(chunk_id == c, m, m_prev)
            m = jnp.maximum(tot[c:c + 1, :] + m, gmax[c:c + 1, :])
            m_new = jnp.where(chunk_id == c, m, m_new)
        gl_ref[d, U] = u
        gl_ref[d, NLF] = -lf
        gl_ref[d, MPREV] = m_prev
        gl_ref[d, WROW] = jnp.exp2(u + tot - m_new)
        gl_ref[d, DECAY] = jnp.exp2(tot + m_prev - m_new)

    def conv_chunk(c, src_ref, half_taps, dst_ref, scale, slot):
        r0 = c * CHUNK
        if c == 0:
            edge_ref[slot, 0:8, :] = jnp.zeros((8, LANES), F32)
            edge_ref[slot, 8:8 + CHUNK + 8, :] = src_ref[0:CHUNK + 8, :]
            window = lambda off: edge_ref[slot, 8 + off:8 + off + CHUNK, :]
        elif c == n_chunks - 1:
            edge_ref[slot + 1, 0:CHUNK + 8, :] = src_ref[seq - CHUNK - 8:seq, :]
            edge_ref[slot + 1, CHUNK + 8:CHUNK + 16, :] = jnp.zeros((8, LANES), F32)
            window = lambda off: edge_ref[slot + 1, 8 + off:8 + off + CHUNK, :]
        else:
            window = lambda off: src_ref[r0 + off:r0 + off + CHUNK, :]
        h = half_taps[0:1, :] * window(-pad)
        for t in range(1, CONV_WIDTH):
            h = h + half_taps[t:t + 1, :] * window(t - pad)
        act = h + h * jnp.tanh(h)
        if scale != 1.0:
            act = act * scale
        dst_ref[r0:r0 + CHUNK, :] = act.astype(dst_ref.dtype)

    half_q, half_k = 0.5 * cwq_ref[...], 0.5 * cwk_ref[...]
    q_conv = lambda c: conv_chunk(c, q_ref, half_q, qc_ref, 1.0, 0)
    k_conv = lambda c: conv_chunk(c, k_ref, half_k, kc_ref, 1.0 / math.sqrt(M_HEAD_DIM), 2)

    cst_ref[...] = jnp.zeros(cst_ref.shape, F32)

    def state_step(i):
        for d in range(2):
            c = i if d == 0 else n_chunks - 1 - i
            rows = pl.ds(c * CHUNK, CHUNK)
            w = gl_ref[d, WROW, c:c + 1, :]
            decay = gl_ref[d, DECAY, c:c + 1, :]
            upd = jnp.dot((kc_ref[rows, :].T * w).astype(BF16), va_ref[rows, :],
                          preferred_element_type=F32)
            state = cst_ref[d]
            cprev_ref[d, c] = state.astype(BF16)
            cst_ref[d] = jnp.concatenate([decay, decay], axis=1) * state + upd

    row = lax.broadcasted_iota(jnp.int32, (CHUNK, CHUNK), 0)
    col = lax.broadcasted_iota(jnp.int32, (CHUNK, CHUNK), 1)
    masks = (col <= row, col >= row)

    def out_chunk(c):
        rows = pl.ds(c * CHUNK, CHUNK)
        q = qc_ref[rows, :]
        va = va_ref[rows, :]
        s = lax.dot_general(q, kc_ref[rows, :].astype(BF16), (((1,), (1,)), ((), ())), preferred_element_type=F32)
        hm = jnp.zeros((CHUNK, LANES), F32)
        for d in range(2):
            mask = masks[d]
            u = gl_ref[d, U, c:c + 1, :]
            nlf = gl_ref[d, NLF, c:c + 1, :]
            m_prev = gl_ref[d, MPREV, c:c + 1, :]
            um = jnp.where(mask, u, NEG_INF)
            mloc = jnp.maximum(jnp.max(um, axis=-1, keepdims=True), m_prev)
            dt = jnp.exp2(um - mloc)
            inter = jnp.exp2(m_prev - mloc)
            lhs = jnp.concatenate([(s * dt).astype(BF16), q * inter.astype(BF16)], axis=1)
            rhs = jnp.concatenate([va, cprev_ref[d, c]], axis=0)
            out = jnp.dot(lhs, rhs, preferred_element_type=F32)
            nb_col = jnp.sum(jnp.where(mask, nlf, 0.0), axis=-1, keepdims=True)
            stab = jnp.exp2(nb_col - mloc)
            hm = hm + out[:, 0:LANES] / jnp.maximum(jnp.abs(out[:, LANES:]), stab)
        ms = jnp.mean(hm * hm, axis=-1, keepdims=True)
        y = hm * lax.rsqrt(ms + NORM_EPS) * o_ref[rows, :]
        y_ref[rows, :] = y.astype(y_ref.dtype)

    for conv in (q_conv, k_conv):
        for c in (0, n_chunks - 1, *range(1, n_chunks - 1)):
            conv(c)
    for i in range(n_chunks):
        state_step(i)
    for c in range(n_chunks):
        out_chunk(c)


def _mlstm(slabs, va, gates_t, gate_bias, conv_w, layer, batch, seq):
    n_chunks = seq // CHUNK
    col = lambda off: pl.BlockSpec((None, seq, LANES), lambda b, h: (off + h, b, 0))
    return pl.pallas_call(
        functools.partial(_mlstm_kernel, bias_base=layer * N_GATES),
        grid=(batch, M_HEADS),
        in_specs=[
            pl.BlockSpec(memory_space=pltpu.SMEM),
            col(SLAB_MQ), col(SLAB_MK),
            pl.BlockSpec((None, seq, 2 * LANES), lambda b, h: (h, b, 0)),
            col(SLAB_MO),
            pl.BlockSpec((4, None, n_chunks, LANES), lambda b, h: (h, b, 0, 0)),
            pl.BlockSpec((None, CONV_WIDTH, LANES), lambda b, h: (layer, 0, h)),
            pl.BlockSpec((None, CONV_WIDTH, LANES), lambda b, h: (layer, 0, M_HEADS + h)),
        ],
        out_specs=pl.BlockSpec((None, seq, LANES), lambda b, h: (b, 0, h)),
        out_shape=jax.ShapeDtypeStruct((batch, seq, M_WIDTH), BF16),
        scratch_shapes=[
            pltpu.VMEM((4, CHUNK + 16, LANES), F32),
            pltpu.VMEM((seq, LANES), BF16),
            pltpu.VMEM((seq, LANES), F32),
            pltpu.VMEM((2, M_HEAD_DIM, 2 * LANES), F32),
            pltpu.VMEM((2, n_chunks, M_HEAD_DIM, 2 * LANES), BF16),
            pltpu.VMEM((2, N_GATE_ROWS, n_chunks, LANES), F32),
        ],
        compiler_params=pltpu.CompilerParams(
            dimension_semantics=("arbitrary", "arbitrary"), vmem_limit_bytes=VMEM_LIMIT),
        name="mlstm",
    )(gate_bias, slabs, slabs, va, slabs, gates_t,
      conv_w, conv_w)


def _attn_kernel(q_ref, k_ref, v_ref, bias_ref, bias8_ref, y_ref,
                 c4_ref, c8_ref, num_ref, den_ref, mx_ref):
    seq = q_ref.shape[0]
    rope_rows = 256

    d4, d16 = DILATIONS[1], DILATIONS[2]
    d8 = d16 // 2
    len4, len8 = seq // d4, seq // d8

    def split4(r, l0):
        for t, src in enumerate((q_ref, k_ref, v_ref)):
            c4_ref[t, r * len4 + l0:r * len4 + l0 + rope_rows, :] = src[pl.ds(r + d4 * l0, rope_rows, stride=d4), :]

    def split8(r):
        for t in range(3):
            c8_ref[t, r * len8:(r + 1) * len8, :] = c4_ref[t, pl.ds((r % d4) * len4 + r // d4, len8, stride=d8 // d4), :]

    lane_q = lax.broadcasted_iota(jnp.int32, (QB, LANES), 1)
    head0_q = lane_q < A_HEAD_DIM

    def block(q, k, v, bias):
        span = k.shape[0]
        q2 = jnp.concatenate([jnp.where(head0_q, q, 0.0), jnp.where(head0_q, 0.0, q)], axis=0).astype(BF16)
        s = lax.dot_general(q2, k.astype(BF16), (((1,), (1,)), ((), ())), preferred_element_type=F32)
        s = s + jnp.concatenate([bias, bias], axis=0)
        m = jnp.max(s, axis=-1, keepdims=True)
        p = jnp.exp2(s - m).astype(BF16)
        rhs = jnp.concatenate([v.astype(BF16), jnp.ones((span, LANES), BF16)], axis=1)
        acc = jnp.dot(p, rhs, preferred_element_type=F32)
        return (jnp.where(head0_q, acc[0:QB, 0:LANES], acc[QB:, 0:LANES]),
                jnp.where(head0_q, acc[0:QB, LANES:], acc[QB:, LANES:]),
                jnp.where(head0_q, m[0:QB], m[QB:2 * QB]))

    stats_refs = (num_ref, den_ref, mx_ref)

    nat = (lambda rows: q_ref[rows, :], lambda rows: k_ref[rows, :], lambda rows: v_ref[rows, :])
    cm4 = tuple((lambda rows, t=t: c4_ref[t, rows, :]) for t in range(3))
    cm8 = tuple((lambda rows, t=t: c8_ref[t, rows, :]) for t in range(3))
    patterns = ((DILATIONS[0], HALF_WINDOW, nat, bias_ref),
                (d4, HALF_WINDOW, cm4, bias_ref),
                (d8, HALF_WINDOW * (d16 // d8), cm8, bias8_ref))

    def band_block(pat, r, j):
        mod, half, srcs, biases = patterns[pat]
        length = seq // mod
        q0 = j * QB
        k0 = min(max(q0 - half, 0), length - SPAN)
        qrows = pl.ds(r * length + q0, QB)
        krows = pl.ds(r * length + k0, SPAN)
        out_rows = qrows if mod == 1 else pl.ds(r + mod * q0, QB, stride=mod)
        res = block(srcs[0](qrows), srcs[1](krows), srcs[2](krows), biases[(q0 - k0) // half])
        for ref, val in zip(stats_refs, res):
            ref[pat, out_rows, :] = val

    def merge(c):
        rows = pl.ds(c * rope_rows, rope_rows)
        m_all = jnp.maximum(jnp.maximum(mx_ref[0, rows, :], mx_ref[1, rows, :]), mx_ref[2, rows, :])
        num = jnp.zeros((rope_rows, LANES), F32)
        den = jnp.zeros((rope_rows, LANES), F32)
        for pat in range(len(DILATIONS)):
            w = jnp.exp2(mx_ref[pat, rows, :] - m_all)
            num = num + w * num_ref[pat, rows, :]
            den = den + w * den_ref[pat, rows, :]
        y_ref[rows, :] = (num / den).astype(y_ref.dtype)

    for r in range(d4):
        for l0 in range(0, len4, rope_rows):
            split4(r, l0)
    for r in range(d8):
        split8(r)
    n_groups = len8 // QB
    group_rows = seq // n_groups
    for g in range(n_groups):
        for pat in reversed(range(len(patterns))):
            mod = patterns[pat][0]
            per_group = (seq // mod // QB) // n_groups
            for r in range(mod):
                for j in range(g * per_group, (g + 1) * per_group):
                    band_block(pat, r, j)
        for c in range(g * group_rows // rope_rows, (g + 1) * group_rows // rope_rows):
            merge(c)


def _attention(slabs, bias, bias8, batch, seq):
    assert DILATIONS == (1, 4, 16) and seq * 2 // DILATIONS[2] == SPAN
    n_pairs = A_WIDTH // LANES
    col = lambda off: pl.BlockSpec((None, seq, LANES), lambda b, p: (off + p, b, 0))
    const2 = lambda shape: pl.BlockSpec(shape, lambda b, p: (0,) * len(shape))
    n_pat = len(DILATIONS)
    return pl.pallas_call(
        _attn_kernel,
        grid=(batch, n_pairs),
        in_specs=[
            col(SLAB_AQ), col(SLAB_AK), col(SLAB_AV),
            const2((3, QB, SPAN)), const2((2, QB, SPAN)),
        ],
        out_specs=pl.BlockSpec((None, seq, LANES), lambda b, p: (b, 0, p)),
        out_shape=jax.ShapeDtypeStruct((batch, seq, A_WIDTH), BF16),
        scratch_shapes=[
            pltpu.VMEM((3, seq, LANES), F32),
            pltpu.VMEM((3, seq, LANES), F32),
            pltpu.VMEM((n_pat, seq, LANES), F32),
            pltpu.VMEM((n_pat, seq, LANES), F32),
            pltpu.VMEM((n_pat, seq, LANES), F32),
        ],
        compiler_params=pltpu.CompilerParams(
            dimension_semantics=("arbitrary", "arbitrary"), vmem_limit_bytes=VMEM_LIMIT),
        name="dilated_attn",
    )(slabs, slabs, slabs, bias, bias8)


def _out_mlp_kernel(x_ref, ym_ref, ya_ref, wo_ref, g2_ref, wu_ref, wd_ref, fg_ref, o_ref, usq_ref,
                    *, ff_chunk, final):
    x1 = (x_ref[...]
          + jnp.dot(ym_ref[...], wo_ref[0:M_WIDTH, :], preferred_element_type=F32)
          + jnp.dot(ya_ref[...], wo_ref[M_WIDTH:, :], preferred_element_type=F32))
    ms = jnp.mean(x1 * x1, axis=-1, keepdims=True)
    h = (x1 * lax.rsqrt(ms + NORM_EPS) * g2_ref[...]).astype(BF16)
    for c in range(0, D_FF, ff_chunk):
        u = jnp.maximum(jnp.dot(h, wu_ref[:, c:c + ff_chunk], preferred_element_type=F32), 0.0)
        usq_ref[:, c:c + ff_chunk] = (u * u).astype(BF16)
    out = x1 + jnp.dot(usq_ref[...], wd_ref[...], preferred_element_type=F32)
    if final:
        ms = jnp.mean(out * out, axis=-1, keepdims=True)
        out = out * lax.rsqrt(ms + NORM_EPS) * fg_ref[...]
    o_ref[...] = out


def _out_mlp(x2d, ym, ya, w_out, g2, w_up, w_down, final_g, layer, final, tm=512):
    m = x2d.shape[0]
    resident = lambda shape: pl.BlockSpec((None,) + shape, lambda i: (layer, 0, 0), pipeline_mode=pl.Buffered(1))
    return pl.pallas_call(
        functools.partial(_out_mlp_kernel, ff_chunk=1024, final=final),
        grid=(m // tm,),
        in_specs=[
            pl.BlockSpec((tm, D_MODEL), lambda i: (i, 0)),
            pl.BlockSpec((tm, M_WIDTH), lambda i: (i, 0)),
            pl.BlockSpec((tm, A_WIDTH), lambda i: (i, 0)),
            resident((D_MODEL, D_MODEL)),
            resident((1, D_MODEL)),
            resident((D_MODEL, D_FF)),
            resident((D_FF, D_MODEL)),
            pl.BlockSpec((1, D_MODEL), lambda i: (0, 0), pipeline_mode=pl.Buffered(1)),
        ],
        out_specs=pl.BlockSpec((tm, D_MODEL), lambda i: (i, 0)),
        out_shape=jax.ShapeDtypeStruct((m, D_MODEL), F32),
        scratch_shapes=[pltpu.VMEM((tm, D_FF), BF16)],
        compiler_params=pltpu.CompilerParams(
            dimension_semantics=("arbitrary",), vmem_limit_bytes=VMEM_LIMIT),
        name="out_mlp",
    )(x2d, ym, ya, w_out, g2, w_up, w_down, final_g)


def _rope_tables(seq):
    half = A_HEAD_DIM // 2
    f32 = np.float32
    inv_freq = f32(ROPE_THETA) ** (-np.arange(half, dtype=f32) / f32(half))
    ang = np.arange(seq, dtype=f32)[:, None] * inv_freq[None, :]
    cos = np.tile(np.cos(ang), (1, LANES // half)).astype(f32)
    sin = np.sin(ang).astype(f32)
    sin = np.concatenate([-sin, sin, -sin, sin], axis=1)
    q_scale = f32(LOG2E / math.sqrt(A_HEAD_DIM))
    return jnp.asarray(np.concatenate([cos * q_scale, sin * q_scale, cos, sin], axis=1))


def _band_bias():
    row = np.arange(QB)[:, None]
    col = np.arange(SPAN)[None, :]
    variants = []
    for off in (0, HALF_WINDOW, 2 * HALF_WINDOW):
        rel = col - off - row
        variants.append(np.where(np.abs(rel) <= HALF_WINDOW, 0.0, NEG_INF))
    variants8 = []
    for off in (0, QB):
        rel = col - off - row
        variants8.append(np.where((np.abs(rel) <= 2 * HALF_WINDOW) & (rel % 2 == 0), 0.0, NEG_INF))
    return (jnp.asarray(np.stack(variants).astype(np.float32)),
            jnp.asarray(np.stack(variants8).astype(np.float32)))


def kernel(x, norm1_g, w_in, conv_w, gate_i_b, gate_f_b, head_norm_g, w_out, norm2_g, w_up, w_down, final_g):
    batch, seq, _ = x.shape
    depth = w_in.shape[0]
    n_chunks = seq // CHUNK
    rope_tab = _rope_tables(seq)
    bias, bias8 = _band_bias()

    gate_lo = 4 * M_WIDTH
    attn_lo = gate_lo + N_GATES
    w_in_b = w_in.astype(BF16)
    w_gates = jnp.swapaxes(w_in_b[:, :, gate_lo:attn_lo].reshape(depth, D_MODEL, 4, M_HEADS), 2, 3)
    w_gates = jnp.pad(w_gates.reshape(depth, D_MODEL, N_GATES), ((0, 0), (0, 0), (0, LANES - N_GATES)))
    w_groups = (w_in_b[:, :, :gate_lo], w_in_b[:, :, attn_lo:], w_gates)
    w_out_b, w_up_b, w_down_b = w_out.astype(BF16), w_up.astype(BF16), w_down.astype(BF16)
    gate_bias = jnp.concatenate([gate_i_b, gate_f_b], axis=1).reshape(depth * N_GATES).astype(F32)
    norm1_3, norm2_3, head_g3 = norm1_g[:, None, :], norm2_g[:, None, :], head_norm_g[:, None, :]

    x2d = x.reshape(batch * seq, D_MODEL)
    for layer in range(depth):
        slabs, va, gates_t = _in_proj(x2d, norm1_3, head_g3, w_groups, rope_tab, layer, seq)
        ym = _mlstm(slabs, va, gates_t.reshape(LANES, batch, n_chunks, CHUNK), gate_bias, conv_w, layer, batch, seq)
        ya = _attention(slabs, bias, bias8, batch, seq)
        x2d = _out_mlp(x2d, ym.reshape(batch * seq, M_WIDTH), ya.reshape(batch * seq, A_WIDTH),
                       w_out_b, norm2_3, w_up_b, w_down_b, final_g[None, :],
                       layer, final=(layer == depth - 1))
    return x2d.reshape(batch, seq, D_MODEL)
```
